```python
import jax
import jax.numpy as jnp
from jax import lax
import numpy as np

D_MODEL = 4096
BATCH = 4
SEQ = 4096
DEPTH = 1

GRID_W = 64
CTX_LEN = 256
W_LRU = 2048
N_LRU_HEADS = 16
LRU_HEAD_DIM = W_LRU // N_LRU_HEADS
CONV_WIDTH = 4
CONV_PAD_LEFT = 1
CONV_PAD_RIGHT = CONV_WIDTH - 1 - CONV_PAD_LEFT
LRU_C = 8.0
W_SG = 2048
N_SG_GROUPS = 16
SG_GROUP_DIM = W_SG // N_SG_GROUPS
ROWS_PER_CHUNK = 2
CHUNK = ROWS_PER_CHUNK * GRID_W
N_EXPERTS = 32
TOP_K = 4
D_EXPERT = 1024
SWIGLU_LIMIT = 7.0
SWIGLU_ALPHA = 1.702
MOE_BLOCK = 128
DN_ALPHA = (2.0 * DEPTH) ** 0.25
DN_BETA = (8.0 * DEPTH) ** -0.25
LN_EPS = 1e-5
N_MOD = 6
IN_SPLITS = (W_LRU, W_LRU, W_SG, W_SG, D_MODEL, D_MODEL)
N_IN = W_LRU * 2 + W_SG * 2 + D_MODEL * 2

kernel_name = 'hybrid_rglru_sgmlp_moe_diffusion_block'


def _layer_norm(x, g, b):
    xf = x.astype(jnp.float32)
    mu = jnp.mean(xf, axis=-1, keepdims=True)
    xc = xf - mu
    var = jnp.mean(xc * xc, axis=-1, keepdims=True)
    y = xc * lax.rsqrt(var + LN_EPS)
    return (y * g.astype(jnp.float32) + b.astype(jnp.float32)).astype(x.dtype)


def _split_in(z):
    offs = []
    acc = 0
    for w in IN_SPLITS[:-1]:
        acc += w
        offs.append(acc)
    return jnp.split(z, offs, axis=-1)


def _short_conv(u, w, b):
    length = u.shape[1]
    up = jnp.pad(u, ((0, 0), (CONV_PAD_LEFT, CONV_PAD_RIGHT), (0, 0)))
    out = b
    for k in range(CONV_WIDTH):
        out = out + up[:, k:k + length] * w[k]
    return out


def _block_diag(u, w, b):
    bsz, length, _ = u.shape
    y = jnp.einsum('blhi,hij->blhj', u.reshape(bsz, length, N_LRU_HEADS, LRU_HEAD_DIM), w)
    return y.reshape(bsz, length, W_LRU) + b


def _linear_scan(a, b, h0, reverse):
    def combine(left, right):
        a_l, b_l = left
        a_r, b_r = right
        return a_l * a_r, a_r * b_l + b_r
    a_cum, b_cum = lax.associative_scan(combine, (a, b), reverse=reverse, axis=1)
    return b_cum + a_cum * h0[:, None, :]


def _rglru_scan(u, wa, ba, wx, bx, lam, h0, reverse):
    uf = u.astype(jnp.float32)
    r = jax.nn.sigmoid(_block_diag(u, wa, ba).astype(jnp.float32))
    i = jax.nn.sigmoid(_block_diag(u, wx, bx).astype(jnp.float32))
    log_a = -LRU_C * r * jax.nn.softplus(-lam.astype(jnp.float32))
    a = jnp.exp(log_a)
    mult = jnp.sqrt(-jnp.expm1(2.0 * log_a))
    return _linear_scan(a, mult * i * uf, h0, reverse)


def _spatial_gating(u, v, ln_g, ln_b, sw, sb, n_chunks):
    bsz, length, _ = v.shape
    u = jax.nn.gelu(u)
    v = _layer_norm(jax.nn.gelu(v), ln_g, ln_b)
    v = v.reshape(bsz, n_chunks, CHUNK, N_SG_GROUPS, SG_GROUP_DIM)
    mixed = jnp.einsum('gts,bcsgd->bctgd', sw, v) + sb.T[:, :, None]
    return u * mixed.reshape(bsz, length, W_SG)


def _merge(y_a, y_b, z_ga, z_gb, wa, wb, wo):
    merged = jax.nn.sigmoid(z_ga) * (y_a @ wa) + jax.nn.sigmoid(z_gb) * (y_b @ wb)
    return merged @ wo


def _clamped_swiglu(gu):
    x_glu = jnp.minimum(gu[..., ::2], SWIGLU_LIMIT)
    x_lin = jnp.clip(gu[..., 1::2], -SWIGLU_LIMIT, SWIGLU_LIMIT)
    return x_glu * jax.nn.sigmoid(SWIGLU_ALPHA * x_glu) * (x_lin + 1.0)


def _moe(tokens, w_r, b_r, w_gu, b_gu, w_dn, b_dn):
    n_tok, d = tokens.shape
    logits = (tokens @ w_r + b_r).astype(jnp.float32)
    top_val, top_idx = lax.top_k(logits, TOP_K)
    gates = jax.nn.softmax(top_val, axis=-1)
    n_assign = n_tok * TOP_K
    n_blocks = -(-(n_assign + N_EXPERTS * (MOE_BLOCK - 1)) // MOE_BLOCK)
    n_pad = n_blocks * MOE_BLOCK
    expert = top_idx.reshape(n_assign)
    token = jnp.repeat(jnp.arange(n_tok, dtype=jnp.int32), TOP_K)
    gate = gates.reshape(n_assign)
    order = jnp.argsort(expert)
    s_exp = expert[order]
    s_tok = token[order]
    s_gate = gate[order]
    counts = jnp.bincount(expert, length=N_EXPERTS)
    padded = (counts + MOE_BLOCK - 1) // MOE_BLOCK * MOE_BLOCK
    grp_start = jnp.cumsum(counts) - counts
    pad_end = jnp.cumsum(padded)
    pad_start = pad_end - padded
    dest = pad_start[s_exp] + jnp.arange(n_assign, dtype=jnp.int32) - grp_start[s_exp]
    tok_pad = jnp.zeros((n_pad,), jnp.int32).at[dest].set(s_tok)
    gate_pad = jnp.zeros((n_pad,), jnp.float32).at[dest].set(s_gate)
    blk_start = jnp.arange(n_blocks, dtype=pad_end.dtype) * MOE_BLOCK
    blk_exp = jnp.minimum(jnp.searchsorted(pad_end, blk_start, side='right'), N_EXPERTS - 1)

    def run_block(args):
        e, tok, g = args
        xb = tokens[tok]
        act = _clamped_swiglu(xb @ w_gu[e] + b_gu[e])
        yb = act @ w_dn[e] + b_dn[e]
        return yb.astype(jnp.float32) * g[:, None]

    ys = lax.map(run_block, (blk_exp, tok_pad.reshape(n_blocks, MOE_BLOCK),
                             gate_pad.reshape(n_blocks, MOE_BLOCK)))
    out = jax.ops.segment_sum(ys.reshape(n_pad, d), tok_pad, num_segments=n_tok)
    return out.astype(tokens.dtype)


def setup_inputs(seed: int = 0) -> dict:
    key = jax.random.key(seed)
    keys = list(jax.random.split(key, 40))
    f32 = jnp.float32

    def nrm(shape, scale):
        return jax.random.normal(keys.pop(), shape, f32) * scale

    L = DEPTH
    D = D_MODEL
    u_lam = jax.random.uniform(keys.pop(), (L, 2, W_LRU), f32, 0.9, 0.999)
    a_lam = u_lam ** (1.0 / LRU_C)
    lru_lambda = jnp.log(a_lam) - jnp.log1p(-a_lam)
    return {
        'x': nrm((BATCH, SEQ, D), 1.0),
        'c': nrm((BATCH, D), 1.0),
        'ctx': nrm((BATCH, CTX_LEN, D), 1.0),
        'c_ctx': nrm((D,), 1.0),
        'w_mod': nrm((L, D, N_MOD * D), 0.2 * D ** -0.5),
        'b_mod': nrm((L, N_MOD * D), 0.02),
        'w_in': nrm((L, D, N_IN), D ** -0.5),
        'conv_w': nrm((L, CONV_WIDTH, W_LRU), CONV_WIDTH ** -0.5),
        'conv_b': nrm((L, W_LRU), 0.02),
        'lru_wa': nrm((L, 2, N_LRU_HEADS, LRU_HEAD_DIM, LRU_HEAD_DIM), LRU_HEAD_DIM ** -0.5),
        'lru_ba': nrm((L, 2, W_LRU), 0.1),
        'lru_wx': nrm((L, 2, N_LRU_HEADS, LRU_HEAD_DIM, LRU_HEAD_DIM), LRU_HEAD_DIM ** -0.5),
        'lru_bx': nrm((L, 2, W_LRU), 0.1),
        'lru_lambda': lru_lambda,
        'sg_ln_g': 1.0 + nrm((L, W_SG), 0.05),
        'sg_ln_b': nrm((L, W_SG), 0.02),
        'sg_w': nrm((L, N_SG_GROUPS, CHUNK, CHUNK), CHUNK ** -0.5),
        'sg_b': 1.0 + nrm((L, N_SG_GROUPS, CHUNK), 0.1),
        'w_branch_a': nrm((L, W_LRU, D), W_LRU ** -0.5),
        'w_branch_b': nrm((L, W_SG, D), W_SG ** -0.5),
        'w_out': nrm((L, D, D), DN_BETA * D ** -0.5),
        'ln1_g': 1.0 + nrm((L, D), 0.05),
        'ln1_b': nrm((L, D), 0.02),
        'w_router': nrm((L, D, N_EXPERTS), D ** -0.5),
        'b_router': nrm((L, N_EXPERTS), 0.01),
        'w_gate_up': nrm((L, N_EXPERTS, D, 2 * D_EXPERT), D ** -0.5),
        'b_gate_up': nrm((L, N_EXPERTS, 2 * D_EXPERT), 0.02),
        'w_down': nrm((L, N_EXPERTS, D_EXPERT, D), DN_BETA * D_EXPERT ** -0.5),
        'b_down': nrm((L, N_EXPERTS, D), 0.02),
        'ln2_g': 1.0 + nrm((L, D), 0.05),
        'ln2_b': nrm((L, D), 0.02),
    }


def reference(x, c, ctx, c_ctx, w_mod, b_mod, w_in, conv_w, conv_b, lru_wa, lru_ba, lru_wx,
              lru_bx, lru_lambda, sg_ln_g, sg_ln_b, sg_w, sg_b, w_branch_a, w_branch_b, w_out,
              ln1_g, ln1_b, w_router, b_router, w_gate_up, b_gate_up, w_down, b_down,
              ln2_g, ln2_b):
    batch, n_lat, d = x.shape
    rows = n_lat // GRID_W
    n_lat_chunks = rows // ROWS_PER_CHUNK
    n_ctx = ctx.shape[1]
    n_lat_tok = batch * n_lat
    silu_c = jax.nn.silu(c)
    silu_cc = jax.nn.silu(c_ctx)
    xc = ctx
    for l in range(DEPTH):
        last = l == DEPTH - 1
        mod = silu_c @ w_mod[l] + b_mod[l]
        mod_c = silu_cc @ w_mod[l] + b_mod[l]
        sh1, sc1, g1, sh2, sc2, g2 = jnp.split(mod[:, None, :], N_MOD, axis=-1)
        csh1, csc1, cg1, csh2, csc2, cg2 = jnp.split(mod_c, N_MOD, axis=-1)
        h = x * (1.0 + sc1) + sh1
        hc = xc * (1.0 + csc1) + csh1
        wi = w_in[l]

        z_rnn, z_gy, z_u, z_v, z_ga, z_gb = _split_in(h @ wi)
        if last:
            zc_rnn = hc @ wi[:, :W_LRU]
        else:
            zc_rnn, zc_gy, zc_u, zc_v, zc_ga, zc_gb = _split_in(hc @ wi)

        uc = _short_conv(zc_rnn, conv_w[l], conv_b[l])
        ul = _short_conv(z_rnn, conv_w[l], conv_b[l])
        p_f = (lru_wa[l, 0], lru_ba[l, 0], lru_wx[l, 0], lru_bx[l, 0], lru_lambda[l, 0])
        p_b = (lru_wa[l, 1], lru_ba[l, 1], lru_wx[l, 1], lru_bx[l, 1], lru_lambda[l, 1])
        h0 = jnp.zeros((batch, W_LRU), jnp.float32)
        ctx_f = _rglru_scan(uc, *p_f, h0, False)
        ctx_b = _rglru_scan(uc, *p_b, h0, True)
        lat_f = _rglru_scan(ul, *p_f, ctx_f[:, -1], False)
        lat_b = _rglru_scan(ul, *p_b, ctx_b[:, 0], True)
        y_a = (lat_f + lat_b).astype(x.dtype) * jax.nn.gelu(z_gy)

        y_b = _spatial_gating(z_u, z_v, sg_ln_g[l], sg_ln_b[l], sg_w[l], sg_b[l], n_lat_chunks)

        mix = _merge(y_a, y_b, z_ga, z_gb, w_branch_a[l], w_branch_b[l], w_out[l])
        x = _layer_norm(DN_ALPHA * x + g1 * mix, ln1_g[l], ln1_b[l])
        tokens = (x * (1.0 + sc2) + sh2).reshape(n_lat_tok, d)

        if not last:
            yc_a = (ctx_f + ctx_b).astype(xc.dtype) * jax.nn.gelu(zc_gy)
            yc_b = _spatial_gating(zc_u, zc_v, sg_ln_g[l], sg_ln_b[l], sg_w[l], sg_b[l],
                                   n_ctx // CHUNK)
            mix_c = _merge(yc_a, yc_b, zc_ga, zc_gb, w_branch_a[l], w_branch_b[l], w_out[l])
            xc = _layer_norm(DN_ALPHA * xc + cg1 * mix_c, ln1_g[l], ln1_b[l])
            tokens_c = (xc * (1.0 + csc2) + csh2).reshape(batch * n_ctx, d)
            tokens = jnp.concatenate([tokens, tokens_c], axis=0)

        moe = _moe(tokens, w_router[l], b_router[l], w_gate_up[l], b_gate_up[l],
                   w_down[l], b_down[l])
        x = _layer_norm(DN_ALPHA * x + g2 * moe[:n_lat_tok].reshape(batch, n_lat, d),
                        ln2_g[l], ln2_b[l])
        if not last:
            xc = _layer_norm(DN_ALPHA * xc + cg2 * moe[n_lat_tok:].reshape(batch, n_ctx, d),
                             ln2_g[l], ln2_b[l])
    return x
```

```python
import functools

import jax
import jax.numpy as jnp
from jax import lax
from jax.experimental import pallas as pl
from jax.experimental.pallas import tpu as pltpu

TOP_K = 4
LRU_C = 8.0
CONV_WIDTH = 4
CONV_PAD_LEFT = 1
SWIGLU_LIMIT = 7.0
SWIGLU_ALPHA = 1.702
DEPTH = 1
DN_ALPHA = (2.0 * DEPTH) ** 0.25
LN_EPS = 1e-5
N_MOD = 6

SUBLANES = 8
LANES = 128
V7X_VMEM_CAP = 60000 * 1024

F32 = jnp.float32
BF16 = jnp.bfloat16


def _params(semantics, vmem_bytes):
    return pltpu.CompilerParams(dimension_semantics=semantics,
                                vmem_limit_bytes=int(min(vmem_bytes, V7X_VMEM_CAP)))


def _tile(n, pref):
    t = min(n, pref)
    while n % t:
        t //= 2
    return t


def _dot(a, b):
    return jnp.dot(a, b, preferred_element_type=F32)


def _layer_norm_rows(x, g, b):
    mu = jnp.mean(x, axis=-1, keepdims=True)
    xc = x - mu
    var = jnp.mean(xc * xc, axis=-1, keepdims=True)
    return xc * lax.rsqrt(var + LN_EPS) * g + b


def _mod_kernel(c_ref, w_ref, b_ref, o_ref):
    c = c_ref[...]
    s = c * jax.nn.sigmoid(c)
    o_ref[...] = _dot(s.astype(BF16), w_ref[...].astype(BF16)) + b_ref[...]


def _mod(c_rows, w_mod, b_mod):
    m, d = c_rows.shape
    n = w_mod.shape[1]
    tn = _tile(n, 512)
    return pl.pallas_call(
        _mod_kernel,
        grid=(n // tn,),
        in_specs=[pl.BlockSpec((m, d), lambda j: (0, 0)),
                  pl.BlockSpec((d, tn), lambda j: (0, j)),
                  pl.BlockSpec((1, tn), lambda j: (0, j))],
        out_specs=pl.BlockSpec((m, tn), lambda j: (0, j)),
        out_shape=jax.ShapeDtypeStruct((m, n), F32),
        compiler_params=_params(("arbitrary",), 3 * d * tn * 4 + (4 << 20)),
        name="mod",
    )(c_rows, w_mod, b_mod.reshape(1, n))


def _inproj_kernel(x_ref, sc_ref, sh_ref, w_ref, o_ref, h_ref, *, n_plain, n_gelu):
    j = pl.program_id(1)

    @pl.when(j == 0)
    def _():
        h = x_ref[...] * (1.0 + sc_ref[0]) + sh_ref[0]
        h_ref[...] = h.astype(BF16)

    @pl.when(j < n_plain)
    def _():
        o_ref[...] = _dot(h_ref[...], w_ref[...])

    @pl.when(jnp.logical_and(j >= n_plain, j < n_gelu))
    def _():
        o_ref[...] = jax.nn.gelu(_dot(h_ref[...], w_ref[...]))

    @pl.when(j >= n_gelu)
    def _():
        o_ref[...] = jax.nn.sigmoid(_dot(h_ref[...], w_ref[...]))


def _inproj(x2, modv, w_bf, rows_per_batch, mod_row0, n_cols, plain_cols, gelu_cols):
    r, d = x2.shape
    tm = _tile(rows_per_batch, 512)
    tn = _tile(_gcd_all(n_cols, plain_cols, gelu_cols), 1024)
    tiles_per_batch = rows_per_batch // tm
    modv3 = modv.reshape(modv.shape[0], 1, modv.shape[1])

    def mod_map(col):
        return lambda i, j: (mod_row0 + i // tiles_per_batch, 0, col)

    kern = functools.partial(_inproj_kernel, n_plain=plain_cols // tn, n_gelu=gelu_cols // tn)
    vmem = 2 * tm * d * 4 + tm * d * 2 + 2 * d * tn * 2 + 4 * tm * tn * 4 + (4 << 20)
    return pl.pallas_call(
        kern,
        grid=(r // tm, n_cols // tn),
        in_specs=[pl.BlockSpec((tm, d), lambda i, j: (i, 0)),
                  pl.BlockSpec((1, 1, d), mod_map(1)),
                  pl.BlockSpec((1, 1, d), mod_map(0)),
                  pl.BlockSpec((d, tn), lambda i, j: (0, j))],
        out_specs=pl.BlockSpec((tm, tn), lambda i, j: (i, j)),
        out_shape=jax.ShapeDtypeStruct((r, n_cols), F32),
        scratch_shapes=[pltpu.VMEM((tm, d), BF16)],
        compiler_params=_params(("parallel", "arbitrary"), vmem),
        name="inproj",
    )(x2, modv3, modv3, w_bf)


def _gcd_all(*vals):
    import math
    g = 0
    for v in vals:
        g = math.gcd(g, v)
    return g


def _lru_kernel(zr_ref, gy_ref, zc_ref, cw_ref, cb_ref, wax_ref, ba_ref, bx_ref, lam_ref,
                o_ref, hf_ref, hb_ref, *, seq, ctx_len, rows, cb, hd):
    n_heads = cb // hd
    groups = rows // SUBLANES
    cw = cw_ref[...]
    cbias = cb_ref[...]
    row_in_group = lax.broadcasted_iota(jnp.int32, (rows, cb), 0) & (SUBLANES - 1)

    def conv_chunk(ref, r0, total):
        p0 = jnp.maximum(r0 - SUBLANES, 0)
        prev = jnp.where(r0 > 0, ref[pl.ds(pl.multiple_of(p0, SUBLANES), SUBLANES), :], 0.0)
        n0 = jnp.minimum(r0 + rows, total - SUBLANES)
        nxt = jnp.where(r0 + rows < total, ref[pl.ds(pl.multiple_of(n0, SUBLANES), SUBLANES), :], 0.0)
        cur = ref[pl.ds(r0, rows), :]
        ext = jnp.concatenate([prev, cur, nxt], axis=0)
        n_ext = rows + 2 * SUBLANES
        u = cbias + cw[1:2] * cur
        u = u + cw[0:1] * pltpu.roll(ext, 1, axis=0)[SUBLANES:SUBLANES + rows]
        u = u + cw[2:3] * pltpu.roll(ext, n_ext - 1, axis=0)[SUBLANES:SUBLANES + rows]
        u = u + cw[3:4] * pltpu.roll(ext, n_ext - 2, axis=0)[SUBLANES:SUBLANES + rows]
        return u

    def gates(u, direction):
        ub = u.astype(BF16)
        pre_a, pre_x = [], []
        for h in range(n_heads):
            g = _dot(ub[:, h * hd:(h + 1) * hd], wax_ref[direction, h])
            pre_a.append(g[:, :hd])
            pre_x.append(g[:, hd:])
        pre_a = pre_a[0] if n_heads == 1 else jnp.concatenate(pre_a, axis=1)
        pre_x = pre_x[0] if n_heads == 1 else jnp.concatenate(pre_x, axis=1)
        r = jax.nn.sigmoid(pre_a + ba_ref[direction])
        i = jax.nn.sigmoid(pre_x + bx_ref[direction])
        neg_lam = -lam_ref[direction]
        softplus = jnp.maximum(neg_lam, 0.0) + jnp.log1p(jnp.exp(-jnp.abs(neg_lam)))
        log_a = -LRU_C * r * softplus
        a = jnp.exp(log_a)
        mult = jnp.sqrt(jnp.tanh(-log_a) * (1.0 + a * a))
        return a, mult * i * u

    def local_scan(a, b, reverse):
        d = 1
        while d < SUBLANES:
            if reverse:
                shift, ok = rows - d, row_in_group < SUBLANES - d
            else:
                shift, ok = d, row_in_group >= d
            a_s = pltpu.roll(a, shift, axis=0)
            b_s = pltpu.roll(b, shift, axis=0)
            b = b + jnp.where(ok, a * b_s, 0.0)
            a = jnp.where(ok, a * a_s, a)
            d *= 2
        return a, b

    def scan_chunk(ref, r0, total, direction, reverse, h, dst_ref):
        u = conv_chunk(ref, r0, total)
        a, b = gates(u, direction)
        a, b = local_scan(a, b, reverse)
        order = range(groups - 1, -1, -1) if reverse else range(groups)
        for g in order:
            sl = slice(g * SUBLANES, (g + 1) * SUBLANES)
            hg = b[sl] + a[sl] * h
            if dst_ref is not None:
                dst_ref[sl, :] = hg
            h = hg[0:1] if reverse else hg[SUBLANES - 1:SUBLANES]
        return h

    zero = jnp.zeros((1, cb), F32)
    n_ctx_chunks = ctx_len // rows
    n_chunks = seq // rows

    h_f = lax.fori_loop(
        0, n_ctx_chunks,
        lambda c, h: scan_chunk(zc_ref, pl.multiple_of(c * rows, rows), ctx_len, 0, False, h, None),
        zero)
    h_b = lax.fori_loop(
        0, n_ctx_chunks,
        lambda c, h: scan_chunk(zc_ref, pl.multiple_of((n_ctx_chunks - 1 - c) * rows, rows),
                                ctx_len, 1, True, h, None),
        zero)

    def fwd_body(c, h):
        r0 = pl.multiple_of(c * rows, rows)
        h = scan_chunk(zr_ref, r0, seq, 0, False, h, hb_ref)
        hf_ref[pl.ds(r0, rows), :] = hb_ref[...]
        return h

    lax.fori_loop(0, n_chunks, fwd_body, h_f)

    def bwd_body(c, h):
        r0 = pl.multiple_of((n_chunks - 1 - c) * rows, rows)
        h = scan_chunk(zr_ref, r0, seq, 1, True, h, hb_ref)
        y = (hf_ref[pl.ds(r0, rows), :] + hb_ref[...]) * gy_ref[pl.ds(r0, rows), :]
        o_ref[pl.ds(r0, rows), :] = y.astype(BF16)
        return h

    lax.fori_loop(0, n_chunks, bwd_body, h_b)


def _lru(z, zc, conv_w, conv_b, wax, ba, bx, lam, batch, seq, ctx_len, w_lru, hd, gy_col0):
    cb = _tile(w_lru, 2 * hd)
    rows = _tile(_gcd_all(seq, ctx_len), 256)
    n_cb = w_lru // cb
    kern = functools.partial(_lru_kernel, seq=seq, ctx_len=ctx_len, rows=rows, cb=cb, hd=hd)
    vmem = 4 * seq * cb * 4 + 2 * seq * cb * 2 + seq * cb * 4 + 40 * rows * cb * 4 + (4 << 20)
    return pl.pallas_call(
        kern,
        grid=(batch, n_cb),
        in_specs=[pl.BlockSpec((seq, cb), lambda b, j: (b, j)),
                  pl.BlockSpec((seq, cb), lambda b, j: (b, gy_col0 // cb + j)),
                  pl.BlockSpec((ctx_len, cb), lambda b, j: (b, j)),
                  pl.BlockSpec((CONV_WIDTH, cb), lambda b, j: (0, j)),
                  pl.BlockSpec((1, cb), lambda b, j: (0, j)),
                  pl.BlockSpec((2, cb // hd, hd, 2 * hd), lambda b, j: (0, j, 0, 0)),
                  pl.BlockSpec((2, 1, cb), lambda b, j: (0, 0, j)),
                  pl.BlockSpec((2, 1, cb), lambda b, j: (0, 0, j)),
                  pl.BlockSpec((2, 1, cb), lambda b, j: (0, 0, j))],
        out_specs=pl.BlockSpec((seq, cb), lambda b, j: (b, j)),
        out_shape=jax.ShapeDtypeStruct((batch * seq, w_lru), BF16),
        scratch_shapes=[pltpu.VMEM((seq, cb), F32), pltpu.VMEM((rows, cb), F32)],
        compiler_params=_params(("parallel", "parallel"), vmem),
        name="lru",
    )(z, z, zc, conv_w, conv_b.reshape(1, w_lru), wax, ba, bx, lam)


def _sg_kernel(u_ref, v_ref, g_ref, b_ref, sw_ref, sbt_ref, o_ref, *, n_sub, chunk, n_groups, gd):
    vn = _layer_norm_rows(v_ref[...], g_ref[...], b_ref[...]).astype(BF16)
    for c in range(n_sub):
        rs = slice(c * chunk, (c + 1) * chunk)
        for g in range(n_groups):
            cs = slice(g * gd, (g + 1) * gd)
            mixed = _dot(sw_ref[g], vn[rs, cs]) + sbt_ref[:, g:g + 1]
            o_ref[rs, cs] = (u_ref[rs, cs] * mixed).astype(BF16)


def _spatial_gating(z, ln_g, ln_b, sw_bf, sb_t, n_rows, w_sg, u_col0, v_col0):
    n_groups, chunk, _ = sw_bf.shape
    gd = w_sg // n_groups
    n_sub = _tile(n_rows // chunk, 4)
    tm = n_sub * chunk
    kern = functools.partial(_sg_kernel, n_sub=n_sub, chunk=chunk, n_groups=n_groups, gd=gd)
    vmem = 4 * tm * w_sg * 4 + 2 * tm * w_sg * 2 + 6 * tm * w_sg * 4 + (4 << 20)
    return pl.pallas_call(
        kern,
        grid=(n_rows // tm,),
        in_specs=[pl.BlockSpec((tm, w_sg), lambda i: (i, u_col0 // w_sg)),
                  pl.BlockSpec((tm, w_sg), lambda i: (i, v_col0 // w_sg)),
                  pl.BlockSpec((1, w_sg), lambda i: (0, 0)),
                  pl.BlockSpec((1, w_sg), lambda i: (0, 0)),
                  pl.BlockSpec((n_groups, chunk, chunk), lambda i: (0, 0, 0)),
                  pl.BlockSpec((chunk, n_groups), lambda i: (0, 0))],
        out_specs=pl.BlockSpec((tm, w_sg), lambda i: (i, 0)),
        out_shape=jax.ShapeDtypeStruct((n_rows, w_sg), BF16),
        compiler_params=_params(("parallel",), vmem),
        name="sg",
    )(z, z, ln_g.reshape(1, w_sg), ln_b.reshape(1, w_sg), sw_bf, sb_t)


def _merge_kernel(ya_ref, yb_ref, wa_ref, wb_ref, ga_ref, gb_ref, o_ref):
    pa = _dot(ya_ref[...], wa_ref[...])
    pb = _dot(yb_ref[...], wb_ref[...])
    o_ref[...] = (ga_ref[...] * pa + gb_ref[...] * pb).astype(BF16)


def _merge(ya, yb, wa_bf, wb_bf, z, ga_col0, gb_col0):
    r, ka = ya.shape
    kb = yb.shape[1]
    d = wa_bf.shape[1]
    tm = _tile(r, 1024)
    tn = _tile(d, 512)
    vmem = 2 * (tm * (ka + kb) * 2 + (ka + kb) * tn * 2 + 2 * tm * tn * 4 + tm * tn * 2) \
        + 4 * tm * tn * 4 + (4 << 20)
    return pl.pallas_call(
        _merge_kernel,
        grid=(r // tm, d // tn),
        in_specs=[pl.BlockSpec((tm, ka), lambda i, j: (i, 0)),
                  pl.BlockSpec((tm, kb), lambda i, j: (i, 0)),
                  pl.BlockSpec((ka, tn), lambda i, j: (0, j)),
                  pl.BlockSpec((kb, tn), lambda i, j: (0, j)),
                  pl.BlockSpec((tm, tn), lambda i, j: (i, ga_col0 // tn + j)),
                  pl.BlockSpec((tm, tn), lambda i, j: (i, gb_col0 // tn + j))],
        out_specs=pl.BlockSpec((tm, tn), lambda i, j: (i, j)),
        out_shape=jax.ShapeDtypeStruct((r, d), BF16),
        compiler_params=_params(("parallel", "parallel"), vmem),
        name="merge",
    )(ya, yb, wa_bf, wb_bf, z, z)


def _outproj_kernel(m_ref, w_ref, x_ref, g_ref, o_ref):
    o_ref[...] = DN_ALPHA * x_ref[...] + g_ref[0] * _dot(m_ref[...], w_ref[...])


def _outproj(merged, wo_bf, x2, modv, rows_per_batch, g_col):
    r, k = merged.shape
    d = wo_bf.shape[1]
    tm = _tile(rows_per_batch, 1024)
    tn = _tile(d, 1024)
    tiles_per_batch = rows_per_batch // tm
    modv3 = modv.reshape(modv.shape[0], 1, modv.shape[1])
    vmem = 2 * (tm * k * 2 + k * tn * 2 + 2 * tm * tn * 4) + 2 * tm * tn * 4 + (4 << 20)
    return pl.pallas_call(
        _outproj_kernel,
        grid=(r // tm, d // tn),
        in_specs=[pl.BlockSpec((tm, k), lambda i, j: (i, 0)),
                  pl.BlockSpec((k, tn), lambda i, j: (0, j)),
                  pl.BlockSpec((tm, tn), lambda i, j: (i, j)),
                  pl.BlockSpec((1, 1, tn), lambda i, j: (i // tiles_per_batch, 0, g_col * (d // tn) + j))],
        out_specs=pl.BlockSpec((tm, tn), lambda i, j: (i, j)),
        out_shape=jax.ShapeDtypeStruct((r, d), F32),
        compiler_params=_params(("parallel", "parallel"), vmem),
        name="outproj",
    )(merged, wo_bf, x2, modv3)


def _ln_route_kernel(r_ref, g_ref, b_ref, sc_ref, sh_ref, wrt_ref, br_ref,
                     x1_ref, tok_ref, idx_ref, gate_ref, rank_ref, cnt_ref, run_ref, *, n_exp):
    i = pl.program_id(0)
    tm = r_ref.shape[0]

    @pl.when(i == 0)
    def _():
        run_ref[...] = jnp.zeros_like(run_ref)

    x1 = _layer_norm_rows(r_ref[...], g_ref[...], b_ref[...])
    x1_ref[...] = x1
    tok = x1 * (1.0 + sc_ref[0]) + sh_ref[0]
    tok_ref[...] = tok

    logits = lax.dot_general(wrt_ref[...], tok, (((1,), (1,)), ((), ())),
                             precision=lax.Precision.HIGHEST,
                             preferred_element_type=F32) + br_ref[...]
    eidx = lax.broadcasted_iota(jnp.int32, (n_exp, tm), 0)
    vals, idxs = [], []
    work = logits
    for _ in range(TOP_K):
        m = jnp.max(work, axis=0, keepdims=True)
        idx = jnp.min(jnp.where(work == m, eidx, n_exp), axis=0, keepdims=True)
        vals.append(m)
        idxs.append(idx)
        work = jnp.where(eidx == idx, -jnp.inf, work)
    exps = [jnp.exp(v - vals[0]) for v in vals]
    denom = exps[0]
    for e in exps[1:]:
        denom = denom + e
    gate_ref[...] = jnp.concatenate([e / denom for e in exps], axis=0)
    idx_ref[...] = jnp.concatenate(idxs, axis=0)

    sel = [eidx == idx for idx in idxs]
    onehot = sel[0]
    for s in sel[1:]:
        onehot = jnp.logical_or(onehot, s)
    onehot = jnp.where(onehot, 1.0, 0.0)
    s_idx = lax.broadcasted_iota(jnp.int32, (tm, tm), 0)
    t_idx = lax.broadcasted_iota(jnp.int32, (tm, tm), 1)
    strict_upper = jnp.where(s_idx < t_idx, 1.0, 0.0).astype(BF16)
    before = _dot(onehot.astype(BF16), strict_upper) + run_ref[:, 0:1]
    ranks = [jnp.sum(jnp.where(s, before, 0.0), axis=0, keepdims=True) for s in sel]
    rank_ref[...] = jnp.concatenate(ranks, axis=0).astype(jnp.int32)
    run_ref[...] = run_ref[...] + jnp.sum(onehot, axis=1, keepdims=True)
    cnt_ref[...] = run_ref[...].astype(jnp.int32)


def _ln_route(r2, ln_g, ln_b, modv, wr_t, b_r, rows_per_batch, sc_col, sh_col):
    r, d = r2.shape
    n_exp = wr_t.shape[0]
    tm = _tile(rows_per_batch, 256)
    tiles_per_batch = rows_per_batch // tm
    modv3 = modv.reshape(modv.shape[0], 1, modv.shape[1])
    kern = functools.partial(_ln_route_kernel, n_exp=n_exp)
    vmem = 6 * tm * d * 4 + 6 * tm * d * 4 + (8 << 20)
    small = lambda dt: jax.ShapeDtypeStruct((TOP_K, r), dt)
    return pl.pallas_call(
        kern,
        grid=(r // tm,),
        in_specs=[pl.BlockSpec((tm, d), lambda i: (i, 0)),
                  pl.BlockSpec((1, d), lambda i: (0, 0)),
                  pl.BlockSpec((1, d), lambda i: (0, 0)),
                  pl.BlockSpec((1, 1, d), lambda i: (i // tiles_per_batch, 0, sc_col)),
                  pl.BlockSpec((1, 1, d), lambda i: (i // tiles_per_batch, 0, sh_col)),
                  pl.BlockSpec((n_exp, d), lambda i: (0, 0)),
                  pl.BlockSpec((n_exp, 1), lambda i: (0, 0))],
        out_specs=[pl.BlockSpec((tm, d), lambda i: (i, 0)),
                   pl.BlockSpec((tm, d), lambda i: (i, 0)),
                   pl.BlockSpec((TOP_K, tm), lambda i: (0, i)),
                   pl.BlockSpec((TOP_K, tm), lambda i: (0, i)),
                   pl.BlockSpec((TOP_K, tm), lambda i: (0, i)),
                   pl.BlockSpec((n_exp, LANES), lambda i: (0, 0))],
        out_shape=[jax.ShapeDtypeStruct((r, d), F32),
                   jax.ShapeDtypeStruct((r, d), F32),
                   small(jnp.int32), small(F32), small(jnp.int32),
                   jax.ShapeDtypeStruct((n_exp, LANES), jnp.int32)],
        scratch_shapes=[pltpu.VMEM((n_exp, LANES), F32)],
        compiler_params=_params(("arbitrary",), vmem),
        name="ln_route",
    )(r2, ln_g.reshape(1, d), ln_b.reshape(1, d), modv3, modv3, wr_t, b_r.reshape(n_exp, 1))


def _gather_rows_kernel(idx_ref, src_ref, dst_ref, sem, *, tm):
    base = pl.program_id(0) * tm

    def issue(r, carry):
        t = idx_ref[0, 0, r]
        pltpu.make_async_copy(src_ref.at[pl.ds(t, 1)], dst_ref.at[pl.ds(base + r, 1)], sem).start()
        return carry

    lax.fori_loop(0, tm, issue, 0)
    pltpu.make_async_copy(dst_ref.at[pl.ds(base, tm)], dst_ref.at[pl.ds(base, tm)], sem).wait()


def _gather_rows(src, row_idx, tm):
    n = row_idx.shape[0]
    d = src.shape[1]
    kern = functools.partial(_gather_rows_kernel, tm=tm)
    return pl.pallas_call(
        kern,
        grid=(n // tm,),
        in_specs=[pl.BlockSpec((1, 1, tm), lambda i: (i, 0, 0), memory_space=pltpu.SMEM),
                  pl.BlockSpec(memory_space=pl.ANY)],
        out_specs=pl.BlockSpec(memory_space=pl.ANY),
        out_shape=jax.ShapeDtypeStruct((n, d), src.dtype),
        scratch_shapes=[pltpu.SemaphoreType.DMA(())],
        compiler_params=_params(("arbitrary",), 4 << 20),
        name="gather_rows",
    )(row_idx.reshape(n // tm, 1, tm), src)


def _moe_up_kernel(be_ref, nu_ref, x_ref, wg_ref, wl_ref, bg_ref, bl_ref, h_ref):
    i = pl.program_id(0)

    @pl.when(i < nu_ref[0])
    def _():
        x = x_ref[...].astype(BF16)
        glu = jnp.minimum(_dot(x, wg_ref[0]) + bg_ref[0], SWIGLU_LIMIT)
        lin = jnp.clip(_dot(x, wl_ref[0]) + bl_ref[0], -SWIGLU_LIMIT, SWIGLU_LIMIT)
        h_ref[...] = (glu * jax.nn.sigmoid(SWIGLU_ALPHA * glu) * (lin + 1.0)).astype(BF16)

    @pl.when(i >= nu_ref[0])
    def _():
        h_ref[...] = jnp.zeros_like(h_ref)


def _moe_up(xs, blk_exp, n_used, wg_bf, wl_bf, bg, bl, tm):
    n, d = xs.shape
    n_exp, _, f = wg_bf.shape
    vmem = 2 * (tm * d * 4 + 2 * d * f * 2 + tm * f * 2) + 6 * tm * f * 4 + tm * d * 2 + (4 << 20)
    grid_spec = pltpu.PrefetchScalarGridSpec(
        num_scalar_prefetch=2,
        grid=(n // tm,),
        in_specs=[pl.BlockSpec((tm, d), lambda i, be, nu: (i, 0)),
                  pl.BlockSpec((1, d, f), lambda i, be, nu: (be[i], 0, 0)),
                  pl.BlockSpec((1, d, f), lambda i, be, nu: (be[i], 0, 0)),
                  pl.BlockSpec((1, 1, f), lambda i, be, nu: (be[i], 0, 0)),
                  pl.BlockSpec((1, 1, f), lambda i, be, nu: (be[i], 0, 0))],
        out_specs=pl.BlockSpec((tm, f), lambda i, be, nu: (i, 0)))
    return pl.pallas_call(
        _moe_up_kernel,
        grid_spec=grid_spec,
        out_shape=jax.ShapeDtypeStruct((n, f), BF16),
        compiler_params=_params(("arbitrary",), vmem),
        name="moe_up",
    )(blk_exp, n_used, xs, wg_bf, wl_bf, bg.reshape(n_exp, 1, f), bl.reshape(n_exp, 1, f))


def _moe_down_kernel(be_ref, nu_ref, h_ref, w_ref, b_ref, y_ref):
    i = pl.program_id(0)

    @pl.when(i < nu_ref[0])
    def _():
        y_ref[...] = _dot(h_ref[...], w_ref[0]) + b_ref[0]

    @pl.when(i >= nu_ref[0])
    def _():
        y_ref[...] = jnp.zeros_like(y_ref)


def _moe_down(hs, blk_exp, n_used, wd_bf, bd, tm):
    n, f = hs.shape
    n_exp, _, d = wd_bf.shape
    vmem = 2 * (tm * f * 2 + f * d * 2 + tm * d * 4) + 2 * tm * d * 4 + (4 << 20)
    grid_spec = pltpu.PrefetchScalarGridSpec(
        num_scalar_prefetch=2,
        grid=(n // tm,),
        in_specs=[pl.BlockSpec((tm, f), lambda i, be, nu: (i, 0)),
                  pl.BlockSpec((1, f, d), lambda i, be, nu: (be[i], 0, 0)),
                  pl.BlockSpec((1, 1, d), lambda i, be, nu: (be[i], 0, 0))],
        out_specs=pl.BlockSpec((tm, d), lambda i, be, nu: (i, 0)))
    return pl.pallas_call(
        _moe_down_kernel,
        grid_spec=grid_spec,
        out_shape=jax.ShapeDtypeStruct((n, d), F32),
        compiler_params=_params(("arbitrary",), vmem),
        name="moe_down",
    )(blk_exp, n_used, hs, wd_bf, bd.reshape(n_exp, 1, d))


def _combine_kernel(pos_ref, y_ref, gate_ref, x1_ref, g2_ref, lg_ref, lb_ref, o_ref, buf_ref, sem, *, tm):
    def issue(q, carry):
        p = pos_ref[0, 0, q]
        pltpu.make_async_copy(y_ref.at[pl.ds(p, 1)], buf_ref.at[pl.ds(q, 1)], sem).start()
        return carry

    lax.fori_loop(0, TOP_K * tm, issue, 0)
    pltpu.make_async_copy(buf_ref, buf_ref, sem).wait()
    gates = gate_ref[...]
    moe = gates[:, 0:1] * buf_ref[0:tm, :]
    for k in range(1, TOP_K):
        moe = moe + gates[:, k:k + 1] * buf_ref[k * tm:(k + 1) * tm, :]
    pre = DN_ALPHA * x1_ref[...] + g2_ref[0] * moe
    o_ref[...] = _layer_norm_rows(pre, lg_ref[...], lb_ref[...])


def _combine(y, pos, gates_t, x1, modv, ln_g, ln_b, rows_per_batch, g_col):
    r, d = x1.shape
    tm = _tile(rows_per_batch, 128)
    tiles_per_batch = rows_per_batch // tm
    n_steps = r // tm
    modv3 = modv.reshape(modv.shape[0], 1, modv.shape[1])
    pos_steps = pos.reshape(TOP_K, n_steps, tm).transpose(1, 0, 2).reshape(n_steps, 1, TOP_K * tm)
    kern = functools.partial(_combine_kernel, tm=tm)
    vmem = TOP_K * tm * d * 4 + 4 * tm * d * 4 + 4 * tm * d * 4 + (4 << 20)
    return pl.pallas_call(
        kern,
        grid=(n_steps,),
        in_specs=[pl.BlockSpec((1, 1, TOP_K * tm), lambda i: (i, 0, 0), memory_space=pltpu.SMEM),
                  pl.BlockSpec(memory_space=pl.ANY),
                  pl.BlockSpec((tm, TOP_K), lambda i: (i, 0)),
                  pl.BlockSpec((tm, d), lambda i: (i, 0)),
                  pl.BlockSpec((1, 1, d), lambda i: (i // tiles_per_batch, 0, g_col)),
                  pl.BlockSpec((1, d), lambda i: (0, 0)),
                  pl.BlockSpec((1, d), lambda i: (0, 0))],
        out_specs=pl.BlockSpec((tm, d), lambda i: (i, 0)),
        out_shape=jax.ShapeDtypeStruct((r, d), F32),
        scratch_shapes=[pltpu.VMEM((TOP_K * tm, d), F32), pltpu.SemaphoreType.DMA(())],
        compiler_params=_params(("arbitrary",), vmem),
        name="combine",
    )(pos_steps, y, gates_t, x1, modv3, ln_g.reshape(1, d), ln_b.reshape(1, d))


def kernel(x, c, ctx, c_ctx, w_mod, b_mod, w_in, conv_w, conv_b, lru_wa, lru_ba, lru_wx, lru_bx,
           lru_lambda, sg_ln_g, sg_ln_b, sg_w, sg_b, w_branch_a, w_branch_b, w_out, ln1_g, ln1_b,
           w_router, b_router, w_gate_up, b_gate_up, w_down, b_down, ln2_g, ln2_b):
    assert w_mod.shape[0] == DEPTH
    batch, seq, d = x.shape
    ctx_len = ctx.shape[1]
    w_lru = conv_w.shape[-1]
    n_heads = lru_wa.shape[2]
    hd = w_lru // n_heads
    w_sg = sg_ln_g.shape[-1]
    n_exp = w_router.shape[-1]
    n_tok = batch * seq
    l = 0

    pad = (-(batch + 1)) % SUBLANES
    c_rows = jnp.concatenate([c, c_ctx[None, :], jnp.zeros((pad, d), F32)], axis=0)
    modv = _mod(c_rows, w_mod[l], b_mod[l])

    w_in_bf = w_in[l].astype(BF16)
    n_in = w_in_bf.shape[1]
    col_gy, col_u, col_v = w_lru, 2 * w_lru, 2 * w_lru + w_sg
    col_ga = 2 * w_lru + 2 * w_sg
    col_gb = col_ga + d
    x2 = x.reshape(n_tok, d)
    z = _inproj(x2, modv, w_in_bf, seq, 0, n_in, col_gy, col_ga)
    zc = _inproj(ctx.reshape(batch * ctx_len, d), modv, w_in_bf, batch * ctx_len, batch,
                 w_lru, w_lru, w_lru)

    wax = jnp.concatenate([lru_wa[l], lru_wx[l]], axis=-1).astype(BF16)
    y_a = _lru(z, zc, conv_w[l], conv_b[l], wax, lru_ba[l].reshape(2, 1, w_lru),
               lru_bx[l].reshape(2, 1, w_lru), lru_lambda[l].reshape(2, 1, w_lru),
               batch, seq, ctx_len, w_lru, hd, col_gy)
    y_b = _spatial_gating(z, sg_ln_g[l], sg_ln_b[l], sg_w[l].astype(BF16), sg_b[l].T,
                          n_tok, w_sg, col_u, col_v)

    merged = _merge(y_a, y_b, w_branch_a[l].astype(BF16), w_branch_b[l].astype(BF16), z, col_ga, col_gb)
    r1 = _outproj(merged, w_out[l].astype(BF16), x2, modv, seq, 2)
    x1, tokens, idx_t, gate_t, rank_t, counts = _ln_route(
        r1, ln1_g[l], ln1_b[l], modv, w_router[l].T, b_router[l], seq, 4, 3)

    tm_moe = 256
    n_assign = n_tok * TOP_K
    n_tiles = -(-(n_assign + n_exp * (tm_moe - 1)) // tm_moe)
    n_pad = n_tiles * tm_moe
    cnt = counts[:, 0]
    padded = (cnt + tm_moe - 1) // tm_moe * tm_moe
    pad_end = jnp.cumsum(padded)
    pad_start = pad_end - padded
    pos = pad_start[idx_t] + rank_t
    tok_ids = jnp.broadcast_to(jnp.arange(n_tok, dtype=jnp.int32)[None, :], (TOP_K, n_tok))
    tok_pad = jnp.zeros((n_pad,), jnp.int32).at[pos.reshape(-1)].set(tok_ids.reshape(-1))
    blk_start = jnp.arange(n_tiles, dtype=jnp.int32) * tm_moe
    blk_exp = jnp.minimum(jnp.searchsorted(pad_end, blk_start, side='right'), n_exp - 1).astype(jnp.int32)
    n_used = (pad_end[-1] // tm_moe).astype(jnp.int32).reshape(1)

    wgu = w_gate_up[l].astype(BF16)
    xs = _gather_rows(tokens, tok_pad, tm_moe)
    hs = _moe_up(xs, blk_exp, n_used, wgu[..., ::2], wgu[..., 1::2],
                 b_gate_up[l][..., ::2], b_gate_up[l][..., 1::2], tm_moe)
    ys = _moe_down(hs, blk_exp, n_used, w_down[l].astype(BF16), b_down[l], tm_moe)

    out = _combine(ys, pos, gate_t.T, x1, modv, ln2_g[l], ln2_b[l], seq, 5)
    return out.reshape(batch, seq, d)
```

```python
import functools

import jax
import jax.numpy as jnp
from jax import lax
from jax.experimental import pallas as pl
from jax.experimental.pallas import tpu as pltpu

TOP_K = 4
LRU_C = 8.0
CONV_WIDTH = 4
CONV_PAD_LEFT = 1
SWIGLU_LIMIT = 7.0
SWIGLU_ALPHA = 1.702
DEPTH = 1
DN_ALPHA = (2.0 * DEPTH) ** 0.25
LN_EPS = 1e-5
N_MOD = 6

SUBLANES = 8
LANES = 128
V7X_VMEM_CAP = 60000 * 1024

F32 = jnp.float32
BF16 = jnp.bfloat16


def _params(semantics, vmem_bytes):
    return pltpu.CompilerParams(dimension_semantics=semantics,
                                vmem_limit_bytes=int(min(vmem_bytes, V7X_VMEM_CAP)))


def _tile(n, pref):
    t = min(n, pref)
    while n % t:
        t //= 2
    return t


def _dot(a, b):
    return jnp.dot(a, b, preferred_element_type=F32)


def _layer_norm_rows(x, g, b):
    mu = jnp.mean(x, axis=-1, keepdims=True)
    xc = x - mu
    var = jnp.mean(xc * xc, axis=-1, keepdims=True)
    return xc * lax.rsqrt(var + LN_EPS) * g + b


def _mod_kernel(c_ref, w_ref, b_ref, o_ref):
    c = c_ref[...]
    s = c * jax.nn.sigmoid(c)
    o_ref[...] = _dot(s.astype(BF16), w_ref[...].astype(BF16)) + b_ref[...]


def _mod(c_rows, w_mod, b_mod):
    m, d = c_rows.shape
    n = w_mod.shape[1]
    tn = _tile(n, 512)
    return pl.pallas_call(
        _mod_kernel,
        grid=(n // tn,),
        in_specs=[pl.BlockSpec((m, d), lambda j: (0, 0)),
                  pl.BlockSpec((d, tn), lambda j: (0, j)),
                  pl.BlockSpec((1, tn), lambda j: (0, j))],
        out_specs=pl.BlockSpec((m, tn), lambda j: (0, j)),
        out_shape=jax.ShapeDtypeStruct((m, n), F32),
        compiler_params=_params(("arbitrary",), 3 * d * tn * 4 + (4 << 20)),
        name="mod",
    )(c_rows, w_mod, b_mod.reshape(1, n))


def _inproj_kernel(x_ref, sc_ref, sh_ref, w_ref, o_ref, h_ref, *, n_plain, n_gelu):
    j = pl.program_id(1)

    @pl.when(j == 0)
    def _():
        h = x_ref[...] * (1.0 + sc_ref[0]) + sh_ref[0]
        h_ref[...] = h.astype(BF16)

    @pl.when(j < n_plain)
    def _():
        o_ref[...] = _dot(h_ref[...], w_ref[...])

    @pl.when(jnp.logical_and(j >= n_plain, j < n_gelu))
    def _():
        o_ref[...] = jax.nn.gelu(_dot(h_ref[...], w_ref[...]))

    @pl.when(j >= n_gelu)
    def _():
        o_ref[...] = jax.nn.sigmoid(_dot(h_ref[...], w_ref[...]))


def _inproj(x2, modv, w_bf, rows_per_batch, mod_row0, n_cols, plain_cols, gelu_cols):
    r, d = x2.shape
    tm = _tile(rows_per_batch, 512)
    tn = _tile(_gcd_all(n_cols, plain_cols, gelu_cols), 1024)
    tiles_per_batch = rows_per_batch // tm
    modv3 = modv.reshape(modv.shape[0], 1, modv.shape[1])

    def mod_map(col):
        return lambda i, j: (mod_row0 + i // tiles_per_batch, 0, col)

    kern = functools.partial(_inproj_kernel, n_plain=plain_cols // tn, n_gelu=gelu_cols // tn)
    vmem = 2 * tm * d * 4 + tm * d * 2 + 2 * d * tn * 2 + 4 * tm * tn * 4 + (4 << 20)
    return pl.pallas_call(
        kern,
        grid=(r // tm, n_cols // tn),
        in_specs=[pl.BlockSpec((tm, d), lambda i, j: (i, 0)),
                  pl.BlockSpec((1, 1, d), mod_map(1)),
                  pl.BlockSpec((1, 1, d), mod_map(0)),
                  pl.BlockSpec((d, tn), lambda i, j: (0, j))],
        out_specs=pl.BlockSpec((tm, tn), lambda i, j: (i, j)),
        out_shape=jax.ShapeDtypeStruct((r, n_cols), F32),
        scratch_shapes=[pltpu.VMEM((tm, d), BF16)],
        compiler_params=_params(("parallel", "arbitrary"), vmem),
        name="inproj",
    )(x2, modv3, modv3, w_bf)


def _gcd_all(*vals):
    import math
    g = 0
    for v in vals:
        g = math.gcd(g, v)
    return g


def _lru_kernel(zr_ref, gy_ref, zc_ref, cw_ref, cb_ref, wax_ref, ba_ref, bx_ref, lam_ref,
                o_ref, hf_ref, hb_ref, *, seq, ctx_len, rows, cb, hd):
    n_heads = cb // hd
    groups = rows // SUBLANES
    cw = cw_ref[...]
    cbias = cb_ref[...]
    row_in_group = lax.broadcasted_iota(jnp.int32, (rows, cb), 0) & (SUBLANES - 1)

    def conv_chunk(ref, r0, total):
        p0 = jnp.maximum(r0 - SUBLANES, 0)
        prev = jnp.where(r0 > 0, ref[pl.ds(pl.multiple_of(p0, SUBLANES), SUBLANES), :], 0.0)
        n0 = jnp.minimum(r0 + rows, total - SUBLANES)
        nxt = jnp.where(r0 + rows < total, ref[pl.ds(pl.multiple_of(n0, SUBLANES), SUBLANES), :], 0.0)
        cur = ref[pl.ds(r0, rows), :]
        ext = jnp.concatenate([prev, cur, nxt], axis=0)
        n_ext = rows + 2 * SUBLANES
        u = cbias + cw[1:2] * cur
        u = u + cw[0:1] * pltpu.roll(ext, 1, axis=0)[SUBLANES:SUBLANES + rows]
        u = u + cw[2:3] * pltpu.roll(ext, n_ext - 1, axis=0)[SUBLANES:SUBLANES + rows]
        u = u + cw[3:4] * pltpu.roll(ext, n_ext - 2, axis=0)[SUBLANES:SUBLANES + rows]
        return u

    def gates(u, direction):
        ub = u.astype(BF16)
        pre_a, pre_x = [], []
        for h in range(n_heads):
            g = _dot(ub[:, h * hd:(h + 1) * hd], wax_ref[direction, h])
            pre_a.append(g[:, :hd])
            pre_x.append(g[:, hd:])
        pre_a = pre_a[0] if n_heads == 1 else jnp.concatenate(pre_a, axis=1)
        pre_x = pre_x[0] if n_heads == 1 else jnp.concatenate(pre_x, axis=1)
        r = jax.nn.sigmoid(pre_a + ba_ref[direction])
        i = jax.nn.sigmoid(pre_x + bx_ref[direction])
        neg_lam = -lam_ref[direction]
        softplus = jnp.maximum(neg_lam, 0.0) + jnp.log1p(jnp.exp(-jnp.abs(neg_lam)))
        log_a = -LRU_C * r * softplus
        a = jnp.exp(log_a)
        mult = jnp.sqrt(jnp.tanh(-log_a) * (1.0 + a * a))
        return a, mult * i * u

    def local_scan(a, b, reverse):
        d = 1
        while d < SUBLANES:
            if reverse:
                shift, ok = rows - d, row_in_group < SUBLANES - d
            else:
                shift, ok = d, row_in_group >= d
            a_s = pltpu.roll(a, shift, axis=0)
            b_s = pltpu.roll(b, shift, axis=0)
            b = b + jnp.where(ok, a * b_s, 0.0)
            a = jnp.where(ok, a * a_s, a)
            d *= 2
        return a, b

    def scan_chunk(ref, r0, total, direction, reverse, h, dst_ref):
        u = conv_chunk(ref, r0, total)
        a, b = gates(u, direction)
        a, b = local_scan(a, b, reverse)
        order = range(groups - 1, -1, -1) if reverse else range(groups)
        for g in order:
            sl = slice(g * SUBLANES, (g + 1) * SUBLANES)
            hg = b[sl] + a[sl] * h
            if dst_ref is not None:
                dst_ref[sl, :] = hg
            h = hg[0:1] if reverse else hg[SUBLANES - 1:SUBLANES]
        return h

    zero = jnp.zeros((1, cb), F32)
    n_ctx_chunks = ctx_len // rows
    n_chunks = seq // rows

    h_f = lax.fori_loop(
        0, n_ctx_chunks,
        lambda c, h: scan_chunk(zc_ref, pl.multiple_of(c * rows, rows), ctx_len, 0, False, h, None),
        zero)
    h_b = lax.fori_loop(
        0, n_ctx_chunks,
        lambda c, h: scan_chunk(zc_ref, pl.multiple_of((n_ctx_chunks - 1 - c) * rows, rows),
                                ctx_len, 1, True, h, None),
        zero)

    def fwd_body(c, h):
        r0 = pl.multiple_of(c * rows, rows)
        h = scan_chunk(zr_ref, r0, seq, 0, False, h, hb_ref)
        hf_ref[pl.ds(r0, rows), :] = hb_ref[...]
        return h

    lax.fori_loop(0, n_chunks, fwd_body, h_f)

    def bwd_body(c, h):
        r0 = pl.multiple_of((n_chunks - 1 - c) * rows, rows)
        h = scan_chunk(zr_ref, r0, seq, 1, True, h, hb_ref)
        y = (hf_ref[pl.ds(r0, rows), :] + hb_ref[...]) * gy_ref[pl.ds(r0, rows), :]
        o_ref[pl.ds(r0, rows), :] = y.astype(BF16)
        return h

    lax.fori_loop(0, n_chunks, bwd_body, h_b)


def _lru(z, zc, conv_w, conv_b, wax, ba, bx, lam, batch, seq, ctx_len, w_lru, hd, gy_col0):
    cb = _tile(w_lru, 2 * hd)
    rows = _tile(_gcd_all(seq, ctx_len), 256)
    n_cb = w_lru // cb
    kern = functools.partial(_lru_kernel, seq=seq, ctx_len=ctx_len, rows=rows, cb=cb, hd=hd)
    vmem = 4 * seq * cb * 4 + 2 * seq * cb * 2 + seq * cb * 4 + 40 * rows * cb * 4 + (4 << 20)
    return pl.pallas_call(
        kern,
        grid=(batch, n_cb),
        in_specs=[pl.BlockSpec((seq, cb), lambda b, j: (b, j)),
                  pl.BlockSpec((seq, cb), lambda b, j: (b, gy_col0 // cb + j)),
                  pl.BlockSpec((ctx_len, cb), lambda b, j: (b, j)),
                  pl.BlockSpec((CONV_WIDTH, cb), lambda b, j: (0, j)),
                  pl.BlockSpec((1, cb), lambda b, j: (0, j)),
                  pl.BlockSpec((2, cb // hd, hd, 2 * hd), lambda b, j: (0, j, 0, 0)),
                  pl.BlockSpec((2, 1, cb), lambda b, j: (0, 0, j)),
                  pl.BlockSpec((2, 1, cb), lambda b, j: (0, 0, j)),
                  pl.BlockSpec((2, 1, cb), lambda b, j: (0, 0, j))],
        out_specs=pl.BlockSpec((seq, cb), lambda b, j: (b, j)),
        out_shape=jax.ShapeDtypeStruct((batch * seq, w_lru), BF16),
        scratch_shapes=[pltpu.VMEM((seq, cb), F32), pltpu.VMEM((rows, cb), F32)],
        compiler_params=_params(("parallel", "parallel"), vmem),
        name="lru",
    )(z, z, zc, conv_w, conv_b.reshape(1, w_lru), wax, ba, bx, lam)


def _sg_kernel(u_ref, v_ref, g_ref, b_ref, sw_ref, sbt_ref, o_ref, *, n_sub, chunk, n_groups, gd):
    vn = _layer_norm_rows(v_ref[...], g_ref[...], b_ref[...]).astype(BF16)
    for c in range(n_sub):
        rs = slice(c * chunk, (c + 1) * chunk)
        for g in range(n_groups):
            cs = slice(g * gd, (g + 1) * gd)
            mixed = _dot(sw_ref[g], vn[rs, cs]) + sbt_ref[:, g:g + 1]
            o_ref[rs, cs] = (u_ref[rs, cs] * mixed).astype(BF16)


def _spatial_gating(z, ln_g, ln_b, sw_bf, sb_t, n_rows, w_sg, u_col0, v_col0):
    n_groups, chunk, _ = sw_bf.shape
    gd = w_sg // n_groups
    n_sub = _tile(n_rows // chunk, 4)
    tm = n_sub * chunk
    kern = functools.partial(_sg_kernel, n_sub=n_sub, chunk=chunk, n_groups=n_groups, gd=gd)
    vmem = 4 * tm * w_sg * 4 + 2 * tm * w_sg * 2 + 6 * tm * w_sg * 4 + (4 << 20)
    return pl.pallas_call(
        kern,
        grid=(n_rows // tm,),
        in_specs=[pl.BlockSpec((tm, w_sg), lambda i: (i, u_col0 // w_sg)),
                  pl.BlockSpec((tm, w_sg), lambda i: (i, v_col0 // w_sg)),
                  pl.BlockSpec((1, w_sg), lambda i: (0, 0)),
                  pl.BlockSpec((1, w_sg), lambda i: (0, 0)),
                  pl.BlockSpec((n_groups, chunk, chunk), lambda i: (0, 0, 0)),
                  pl.BlockSpec((chunk, n_groups), lambda i: (0, 0))],
        out_specs=pl.BlockSpec((tm, w_sg), lambda i: (i, 0)),
        out_shape=jax.ShapeDtypeStruct((n_rows, w_sg), BF16),
        compiler_params=_params(("parallel",), vmem),
        name="sg",
    )(z, z, ln_g.reshape(1, w_sg), ln_b.reshape(1, w_sg), sw_bf, sb_t)


def _merge_kernel(ya_ref, yb_ref, wa_ref, wb_ref, ga_ref, gb_ref, o_ref):
    pa = _dot(ya_ref[...], wa_ref[...])
    pb = _dot(yb_ref[...], wb_ref[...])
    o_ref[...] = (ga_ref[...] * pa + gb_ref[...] * pb).astype(BF16)


def _merge(ya, yb, wa_bf, wb_bf, z, ga_col0, gb_col0):
    r, ka = ya.shape
    kb = yb.shape[1]
    d = wa_bf.shape[1]
    tm = _tile(r, 1024)
    tn = _tile(d, 512)
    vmem = 2 * (tm * (ka + kb) * 2 + (ka + kb) * tn * 2 + 2 * tm * tn * 4 + tm * tn * 2) \
        + 4 * tm * tn * 4 + (4 << 20)
    return pl.pallas_call(
        _merge_kernel,
        grid=(r // tm, d // tn),
        in_specs=[pl.BlockSpec((tm, ka), lambda i, j: (i, 0)),
                  pl.BlockSpec((tm, kb), lambda i, j: (i, 0)),
                  pl.BlockSpec((ka, tn), lambda i, j: (0, j)),
                  pl.BlockSpec((kb, tn), lambda i, j: (0, j)),
                  pl.BlockSpec((tm, tn), lambda i, j: (i, ga_col0 // tn + j)),
                  pl.BlockSpec((tm, tn), lambda i, j: (i, gb_col0 // tn + j))],
        out_specs=pl.BlockSpec((tm, tn), lambda i, j: (i, j)),
        out_shape=jax.ShapeDtypeStruct((r, d), BF16),
        compiler_params=_params(("parallel", "parallel"), vmem),
        name="merge",
    )(ya, yb, wa_bf, wb_bf, z, z)


def _outproj_kernel(m_ref, w_ref, x_ref, g_ref, o_ref):
    o_ref[...] = DN_ALPHA * x_ref[...] + g_ref[0] * _dot(m_ref[...], w_ref[...])


def _outproj(merged, wo_bf, x2, modv, rows_per_batch, g_col):
    r, k = merged.shape
    d = wo_bf.shape[1]
    tm = _tile(rows_per_batch, 1024)
    tn = _tile(d, 1024)
    tiles_per_batch = rows_per_batch // tm
    modv3 = modv.reshape(modv.shape[0], 1, modv.shape[1])
    vmem = 2 * (tm * k * 2 + k * tn * 2 + 2 * tm * tn * 4) + 2 * tm * tn * 4 + (4 << 20)
    return pl.pallas_call(
        _outproj_kernel,
        grid=(r // tm, d // tn),
        in_specs=[pl.BlockSpec((tm, k), lambda i, j: (i, 0)),
                  pl.BlockSpec((k, tn), lambda i, j: (0, j)),
                  pl.BlockSpec((tm, tn), lambda i, j: (i, j)),
                  pl.BlockSpec((1, 1, tn), lambda i, j: (i // tiles_per_batch, 0, g_col * (d // tn) + j))],
        out_specs=pl.BlockSpec((tm, tn), lambda i, j: (i, j)),
        out_shape=jax.ShapeDtypeStruct((r, d), F32),
        compiler_params=_params(("parallel", "parallel"), vmem),
        name="outproj",
    )(merged, wo_bf, x2, modv3)


def _ln_route_kernel(r_ref, g_ref, b_ref, sc_ref, sh_ref, wrt_ref, br_ref,
                     x1_ref, tok_ref, idx_ref, gate_ref, rank_ref, cnt_ref, run_ref, *, n_exp):
    i = pl.program_id(0)
    tm = r_ref.shape[0]

    @pl.when(i == 0)
    def _():
        run_ref[...] = jnp.zeros_like(run_ref)

    x1 = _layer_norm_rows(r_ref[...], g_ref[...], b_ref[...])
    x1_ref[...] = x1
    tok = x1 * (1.0 + sc_ref[0]) + sh_ref[0]
    tok_ref[...] = tok

    logits = lax.dot_general(wrt_ref[...], tok, (((1,), (1,)), ((), ())),
                             precision=lax.Precision.HIGHEST,
                             preferred_element_type=F32) + br_ref[...]
    eidx = lax.broadcasted_iota(jnp.int32, (n_exp, tm), 0)
    vals, idxs = [], []
    work = logits
    for _ in range(TOP_K):
        m = jnp.max(work, axis=0, keepdims=True)
        idx = jnp.min(jnp.where(work == m, eidx, n_exp), axis=0, keepdims=True)
        vals.append(m)
        idxs.append(idx)
        work = jnp.where(eidx == idx, -jnp.inf, work)
    exps = [jnp.exp(v - vals[0]) for v in vals]
    denom = exps[0]
    for e in exps[1:]:
        denom = denom + e
    gate_ref[...] = jnp.concatenate([e / denom for e in exps], axis=0)
    idx_ref[...] = jnp.concatenate(idxs, axis=0)

    sel = [eidx == idx for idx in idxs]
    onehot = sel[0]
    for s in sel[1:]:
        onehot = jnp.logical_or(onehot, s)
    onehot = jnp.where(onehot, 1.0, 0.0)
    s_idx = lax.broadcasted_iota(jnp.int32, (tm, tm), 0)
    t_idx = lax.broadcasted_iota(jnp.int32, (tm, tm), 1)
    strict_upper = jnp.where(s_idx < t_idx, 1.0, 0.0).astype(BF16)
    before = _dot(onehot.astype(BF16), strict_upper) + run_ref[:, 0:1]
    ranks = [jnp.sum(jnp.where(s, before, 0.0), axis=0, keepdims=True) for s in sel]
    rank_ref[...] = jnp.concatenate(ranks, axis=0).astype(jnp.int32)
    run_ref[...] = run_ref[...] + jnp.sum(onehot, axis=1, keepdims=True)
    cnt_ref[...] = run_ref[...].astype(jnp.int32)


def _ln_route(r2, ln_g, ln_b, modv, wr_t, b_r, rows_per_batch, sc_col, sh_col):
    r, d = r2.shape
    n_exp = wr_t.shape[0]
    tm = _tile(rows_per_batch, 256)
    tiles_per_batch = rows_per_batch // tm
    modv3 = modv.reshape(modv.shape[0], 1, modv.shape[1])
    kern = functools.partial(_ln_route_kernel, n_exp=n_exp)
    vmem = 6 * tm * d * 4 + 6 * tm * d * 4 + (8 << 20)
    small = lambda dt: jax.ShapeDtypeStruct((TOP_K, r), dt)
    return pl.pallas_call(
        kern,
        grid=(r // tm,),
        in_specs=[pl.BlockSpec((tm, d), lambda i: (i, 0)),
                  pl.BlockSpec((1, d), lambda i: (0, 0)),
                  pl.BlockSpec((1, d), lambda i: (0, 0)),
                  pl.BlockSpec((1, 1, d), lambda i: (i // tiles_per_batch, 0, sc_col)),
                  pl.BlockSpec((1, 1, d), lambda i: (i // tiles_per_batch, 0, sh_col)),
                  pl.BlockSpec((n_exp, d), lambda i: (0, 0)),
                  pl.BlockSpec((n_exp, 1), lambda i: (0, 0))],
        out_specs=[pl.BlockSpec((tm, d), lambda i: (i, 0)),
                   pl.BlockSpec((tm, d), lambda i: (i, 0)),
                   pl.BlockSpec((TOP_K, tm), lambda i: (0, i)),
                   pl.BlockSpec((TOP_K, tm), lambda i: (0, i)),
                   pl.BlockSpec((TOP_K, tm), lambda i: (0, i)),
                   pl.BlockSpec((n_exp, LANES), lambda i: (0, 0))],
        out_shape=[jax.ShapeDtypeStruct((r, d), F32),
                   jax.ShapeDtypeStruct((r, d), F32),
                   small(jnp.int32), small(F32), small(jnp.int32),
                   jax.ShapeDtypeStruct((n_exp, LANES), jnp.int32)],
        scratch_shapes=[pltpu.VMEM((n_exp, LANES), F32)],
        compiler_params=_params(("arbitrary",), vmem),
        name="ln_route",
    )(r2, ln_g.reshape(1, d), ln_b.reshape(1, d), modv3, modv3, wr_t, b_r.reshape(n_exp, 1))


def _deinterleave_kernel(w_ref, o_ref, *, f, grp):
    half = grp // 2
    s_idx = lax.broadcasted_iota(jnp.int32, (grp, grp), 0)
    t_idx = lax.broadcasted_iota(jnp.int32, (grp, grp), 1)
    src = jnp.where(t_idx < half, 2 * t_idx, 2 * (t_idx - half) + 1)
    perm = jnp.where(s_idx == src, 1.0, 0.0).astype(BF16)
    for g in range(2 * f // grp):
        p = _dot(w_ref[0, :, g * grp:(g + 1) * grp].astype(BF16), perm)
        o_ref[0, :, g * half:(g + 1) * half] = p[:, :half].astype(BF16)
        o_ref[0, :, f + g * half:f + (g + 1) * half] = p[:, half:].astype(BF16)


def _deinterleave_gate_up(w_gu):
    n_exp, d, f2 = w_gu.shape
    tr = _tile(d, 1024)
    grp = min(f2, 2 * LANES)
    kern = functools.partial(_deinterleave_kernel, f=f2 // 2, grp=grp)
    vmem = 2 * (tr * f2 * 4 + tr * f2 * 2) + 4 * tr * grp * 4 + (4 << 20)
    return pl.pallas_call(
        kern,
        grid=(n_exp, d // tr),
        in_specs=[pl.BlockSpec((1, tr, f2), lambda e, i: (e, i, 0))],
        out_specs=pl.BlockSpec((1, tr, f2), lambda e, i: (e, i, 0)),
        out_shape=jax.ShapeDtypeStruct((n_exp, d, f2), BF16),
        compiler_params=_params(("parallel", "parallel"), vmem),
        name="wgu_prep",
    )(w_gu)


def _moe_up_kernel(be_ref, nu_ref, idx_cur_ref, idx_nxt_ref, tok_ref, w_ref, bg_ref, bl_ref,
                   h_ref, xbuf, sems, *, tm, n_tiles, f):
    i = pl.program_id(0)
    slot = i % 2

    def issue(idx_ref, s):
        def body(r, carry):
            t = idx_ref[0, 0, r]
            pltpu.make_async_copy(tok_ref.at[pl.ds(t, 1)], xbuf.at[s, pl.ds(r, 1)], sems.at[s]).start()
            return carry
        lax.fori_loop(0, tm, body, 0, unroll=8)

    @pl.when(i == 0)
    def _():
        issue(idx_cur_ref, 0)

    @pl.when(i + 1 < n_tiles)
    def _():
        issue(idx_nxt_ref, 1 - slot)

    pltpu.make_async_copy(xbuf.at[slot], xbuf.at[slot], sems.at[slot]).wait()

    @pl.when(i < nu_ref[0])
    def _():
        x = xbuf[slot].astype(BF16)
        glu = jnp.minimum(_dot(x, w_ref[0, :, :f]) + bg_ref[0], SWIGLU_LIMIT)
        lin = jnp.clip(_dot(x, w_ref[0, :, f:]) + bl_ref[0], -SWIGLU_LIMIT, SWIGLU_LIMIT)
        h_ref[...] = (glu * jax.nn.sigmoid(SWIGLU_ALPHA * glu) * (lin + 1.0)).astype(BF16)

    @pl.when(i >= nu_ref[0])
    def _():
        h_ref[...] = jnp.zeros_like(h_ref)


def _moe_up(tokens, row_idx, blk_exp, n_used, wgu_bf, bg, bl, tm):
    n = row_idx.shape[0]
    d = tokens.shape[1]
    n_exp, _, f2 = wgu_bf.shape
    f = f2 // 2
    n_tiles = n // tm
    idx3 = row_idx.reshape(n_tiles, 1, tm)
    kern = functools.partial(_moe_up_kernel, tm=tm, n_tiles=n_tiles, f=f)
    vmem = 2 * tm * d * 4 + 2 * d * f2 * 2 + 2 * tm * f * 2 + 6 * tm * f * 4 + tm * d * 2 + (4 << 20)
    grid_spec = pltpu.PrefetchScalarGridSpec(
        num_scalar_prefetch=2,
        grid=(n_tiles,),
        in_specs=[pl.BlockSpec((1, 1, tm), lambda i, be, nu: (i, 0, 0), memory_space=pltpu.SMEM),
                  pl.BlockSpec((1, 1, tm), lambda i, be, nu: (jnp.minimum(i + 1, n_tiles - 1), 0, 0),
                               memory_space=pltpu.SMEM),
                  pl.BlockSpec(memory_space=pl.ANY),
                  pl.BlockSpec((1, d, f2), lambda i, be, nu: (be[i], 0, 0)),
                  pl.BlockSpec((1, 1, f), lambda i, be, nu: (be[i], 0, 0)),
                  pl.BlockSpec((1, 1, f), lambda i, be, nu: (be[i], 0, 0))],
        out_specs=pl.BlockSpec((tm, f), lambda i, be, nu: (i, 0)),
        scratch_shapes=[pltpu.VMEM((2, tm, d), tokens.dtype), pltpu.SemaphoreType.DMA((2,))])
    return pl.pallas_call(
        kern,
        grid_spec=grid_spec,
        out_shape=jax.ShapeDtypeStruct((n, f), BF16),
        compiler_params=_params(("arbitrary",), vmem),
        name="moe_up",
    )(blk_exp, n_used, idx3, idx3, tokens, wgu_bf, bg.reshape(n_exp, 1, f), bl.reshape(n_exp, 1, f))


def _moe_down_kernel(be_ref, nu_ref, h_ref, w_ref, b_ref, y_ref):
    i = pl.program_id(0)

    @pl.when(i < nu_ref[0])
    def _():
        y_ref[...] = _dot(h_ref[...], w_ref[0]) + b_ref[0]

    @pl.when(i >= nu_ref[0])
    def _():
        y_ref[...] = jnp.zeros_like(y_ref)


def _moe_down(hs, blk_exp, n_used, wd_bf, bd, tm):
    n, f = hs.shape
    n_exp, _, d = wd_bf.shape
    vmem = 2 * (tm * f * 2 + f * d * 2 + tm * d * 4) + 2 * tm * d * 4 + (4 << 20)
    grid_spec = pltpu.PrefetchScalarGridSpec(
        num_scalar_prefetch=2,
        grid=(n // tm,),
        in_specs=[pl.BlockSpec((tm, f), lambda i, be, nu: (i, 0)),
                  pl.BlockSpec((1, f, d), lambda i, be, nu: (be[i], 0, 0)),
                  pl.BlockSpec((1, 1, d), lambda i, be, nu: (be[i], 0, 0))],
        out_specs=pl.BlockSpec((tm, d), lambda i, be, nu: (i, 0)))
    return pl.pallas_call(
        _moe_down_kernel,
        grid_spec=grid_spec,
        out_shape=jax.ShapeDtypeStruct((n, d), F32),
        compiler_params=_params(("arbitrary",), vmem),
        name="moe_down",
    )(blk_exp, n_used, hs, wd_bf, bd.reshape(n_exp, 1, d))


def _combine_kernel(pos_cur_ref, pos_nxt_ref, y_ref, gate_ref, x1_ref, g2_ref, lg_ref, lb_ref,
                    o_ref, buf_ref, sems, *, tm, n_steps):
    i = pl.program_id(0)
    slot = i % 2

    def issue(pos_ref, s):
        def body(q, carry):
            p = pos_ref[0, 0, q]
            pltpu.make_async_copy(y_ref.at[pl.ds(p, 1)], buf_ref.at[s, pl.ds(q, 1)], sems.at[s]).start()
            return carry
        lax.fori_loop(0, TOP_K * tm, body, 0, unroll=8)

    @pl.when(i == 0)
    def _():
        issue(pos_cur_ref, 0)

    @pl.when(i + 1 < n_steps)
    def _():
        issue(pos_nxt_ref, 1 - slot)

    pltpu.make_async_copy(buf_ref.at[slot], buf_ref.at[slot], sems.at[slot]).wait()
    gates = gate_ref[...]
    moe = gates[:, 0:1] * buf_ref[slot, 0:tm, :]
    for k in range(1, TOP_K):
        moe = moe + gates[:, k:k + 1] * buf_ref[slot, k * tm:(k + 1) * tm, :]
    pre = DN_ALPHA * x1_ref[...] + g2_ref[0] * moe
    o_ref[...] = _layer_norm_rows(pre, lg_ref[...], lb_ref[...])


def _combine(y, pos, gates_t, x1, modv, ln_g, ln_b, rows_per_batch, g_col):
    r, d = x1.shape
    tm = _tile(rows_per_batch, 128)
    tiles_per_batch = rows_per_batch // tm
    n_steps = r // tm
    modv3 = modv.reshape(modv.shape[0], 1, modv.shape[1])
    pos_steps = pos.reshape(TOP_K, n_steps, tm).transpose(1, 0, 2).reshape(n_steps, 1, TOP_K * tm)
    kern = functools.partial(_combine_kernel, tm=tm, n_steps=n_steps)
    vmem = 2 * TOP_K * tm * d * 4 + 4 * tm * d * 4 + 4 * tm * d * 4 + (4 << 20)
    return pl.pallas_call(
        kern,
        grid=(n_steps,),
        in_specs=[pl.BlockSpec((1, 1, TOP_K * tm), lambda i: (i, 0, 0), memory_space=pltpu.SMEM),
                  pl.BlockSpec((1, 1, TOP_K * tm), lambda i: (jnp.minimum(i + 1, n_steps - 1), 0, 0),
                               memory_space=pltpu.SMEM),
                  pl.BlockSpec(memory_space=pl.ANY),
                  pl.BlockSpec((tm, TOP_K), lambda i: (i, 0)),
                  pl.BlockSpec((tm, d), lambda i: (i, 0)),
                  pl.BlockSpec((1, 1, d), lambda i: (i // tiles_per_batch, 0, g_col)),
                  pl.BlockSpec((1, d), lambda i: (0, 0)),
                  pl.BlockSpec((1, d), lambda i: (0, 0))],
        out_specs=pl.BlockSpec((tm, d), lambda i: (i, 0)),
        out_shape=jax.ShapeDtypeStruct((r, d), F32),
        scratch_shapes=[pltpu.VMEM((2, TOP_K * tm, d), F32), pltpu.SemaphoreType.DMA((2,))],
        compiler_params=_params(("arbitrary",), vmem),
        name="combine",
    )(pos_steps, pos_steps, y, gates_t, x1, modv3, ln_g.reshape(1, d), ln_b.reshape(1, d))


def kernel(x, c, ctx, c_ctx, w_mod, b_mod, w_in, conv_w, conv_b, lru_wa, lru_ba, lru_wx, lru_bx,
           lru_lambda, sg_ln_g, sg_ln_b, sg_w, sg_b, w_branch_a, w_branch_b, w_out, ln1_g, ln1_b,
           w_router, b_router, w_gate_up, b_gate_up, w_down, b_down, ln2_g, ln2_b):
    assert w_mod.shape[0] == DEPTH
    batch, seq, d = x.shape
    ctx_len = ctx.shape[1]
    w_lru = conv_w.shape[-1]
    n_heads = lru_wa.shape[2]
    hd = w_lru // n_heads
    w_sg = sg_ln_g.shape[-1]
    n_exp = w_router.shape[-1]
    n_tok = batch * seq
    l = 0

    pad = (-(batch + 1)) % SUBLANES
    c_rows = jnp.concatenate([c, c_ctx[None, :], jnp.zeros((pad, d), F32)], axis=0)
    modv = _mod(c_rows, w_mod[l], b_mod[l])

    w_in_bf = w_in[l].astype(BF16)
    n_in = w_in_bf.shape[1]
    col_gy, col_u, col_v = w_lru, 2 * w_lru, 2 * w_lru + w_sg
    col_ga = 2 * w_lru + 2 * w_sg
    col_gb = col_ga + d
    x2 = x.reshape(n_tok, d)
    z = _inproj(x2, modv, w_in_bf, seq, 0, n_in, col_gy, col_ga)
    zc = _inproj(ctx.reshape(batch * ctx_len, d), modv, w_in_bf, batch * ctx_len, batch,
                 w_lru, w_lru, w_lru)

    wax = jnp.concatenate([lru_wa[l], lru_wx[l]], axis=-1).astype(BF16)
    y_a = _lru(z, zc, conv_w[l], conv_b[l], wax, lru_ba[l].reshape(2, 1, w_lru),
               lru_bx[l].reshape(2, 1, w_lru), lru_lambda[l].reshape(2, 1, w_lru),
               batch, seq, ctx_len, w_lru, hd, col_gy)
    y_b = _spatial_gating(z, sg_ln_g[l], sg_ln_b[l], sg_w[l].astype(BF16), sg_b[l].T,
                          n_tok, w_sg, col_u, col_v)

    merged = _merge(y_a, y_b, w_branch_a[l].astype(BF16), w_branch_b[l].astype(BF16), z, col_ga, col_gb)
    r1 = _outproj(merged, w_out[l].astype(BF16), x2, modv, seq, 2)
    x1, tokens, idx_t, gate_t, rank_t, counts = _ln_route(
        r1, ln1_g[l], ln1_b[l], modv, w_router[l].T, b_router[l], seq, 4, 3)

    tm_moe = 256
    n_assign = n_tok * TOP_K
    n_tiles = -(-(n_assign + n_exp * (tm_moe - 1)) // tm_moe)
    n_pad = n_tiles * tm_moe
    cnt = counts[:, 0]
    padded = (cnt + tm_moe - 1) // tm_moe * tm_moe
    pad_end = jnp.cumsum(padded)
    pad_start = pad_end - padded
    expert_ids = jnp.arange(n_exp, dtype=jnp.int32)
    pos = rank_t + jnp.sum(jnp.where(idx_t[..., None] == expert_ids, pad_start, 0), axis=-1)
    tok_ids = jnp.broadcast_to(jnp.arange(n_tok, dtype=jnp.int32)[None, :], (TOP_K, n_tok))
    tok_pad = jnp.zeros((n_pad,), jnp.int32).at[pos.reshape(-1)].set(tok_ids.reshape(-1))
    blk_start = jnp.arange(n_tiles, dtype=jnp.int32) * tm_moe
    blk_exp = jnp.minimum(jnp.sum(blk_start[:, None] >= pad_end[None, :], axis=1), n_exp - 1).astype(jnp.int32)
    n_used = (pad_end[-1] // tm_moe).astype(jnp.int32).reshape(1)

    wgu = _deinterleave_gate_up(w_gate_up[l])
    hs = _moe_up(tokens, tok_pad, blk_exp, n_used, wgu,
                 b_gate_up[l][..., ::2], b_gate_up[l][..., 1::2], tm_moe)
    ys = _moe_down(hs, blk_exp, n_used, w_down[l].astype(BF16), b_down[l], tm_moe)

    out = _combine(ys, pos, gate_t.T, x1, modv, ln2_g[l], ln2_b[l], seq, 5)
    return out.reshape(batch, seq, d)
```

```python
import functools

import jax
import jax.numpy as jnp
from jax import lax
from jax.experimental import pallas as pl
from jax.experimental.pallas import tpu as pltpu

TOP_K = 4
LRU_C = 8.0
CONV_WIDTH = 4
CONV_PAD_LEFT = 1
SWIGLU_LIMIT = 7.0
SWIGLU_ALPHA = 1.702
DEPTH = 1
DN_ALPHA = (2.0 * DEPTH) ** 0.25
LN_EPS = 1e-5
N_MOD = 6

SUBLANES = 8
LANES = 128
V7X_VMEM_CAP = 60000 * 1024

F32 = jnp.float32
BF16 = jnp.bfloat16


def _params(semantics, vmem_bytes):
    return pltpu.CompilerParams(dimension_semantics=semantics,
                                vmem_limit_bytes=int(min(vmem_bytes, V7X_VMEM_CAP)))


def _tile(n, pref):
    t = min(n, pref)
    while n % t:
        t //= 2
    return t


def _dot(a, b):
    return jnp.dot(a, b, preferred_element_type=F32)


def _layer_norm_rows(x, g, b):
    mu = jnp.mean(x, axis=-1, keepdims=True)
    xc = x - mu
    var = jnp.mean(xc * xc, axis=-1, keepdims=True)
    return xc * lax.rsqrt(var + LN_EPS) * g + b


def _mod_kernel(c_ref, w_ref, b_ref, o_ref):
    c = c_ref[...]
    s = c * jax.nn.sigmoid(c)
    o_ref[...] = _dot(s.astype(BF16), w_ref[...].astype(BF16)) + b_ref[...]


def _mod(c_rows, w_mod, b_mod):
    m, d = c_rows.shape
    n = w_mod.shape[1]
    tn = _tile(n, 512)
    return pl.pallas_call(
        _mod_kernel,
        grid=(n // tn,),
        in_specs=[pl.BlockSpec((m, d), lambda j: (0, 0)),
                  pl.BlockSpec((d, tn), lambda j: (0, j)),
                  pl.BlockSpec((1, tn), lambda j: (0, j))],
        out_specs=pl.BlockSpec((m, tn), lambda j: (0, j)),
        out_shape=jax.ShapeDtypeStruct((m, n), F32),
        compiler_params=_params(("arbitrary",), 3 * d * tn * 4 + (4 << 20)),
        name="mod",
    )(c_rows, w_mod, b_mod.reshape(1, n))


def _inproj_kernel(x_ref, sc_ref, sh_ref, w_ref, o_ref, h_ref, *, n_plain, n_gelu):
    j = pl.program_id(1)

    @pl.when(j == 0)
    def _():
        h = x_ref[...] * (1.0 + sc_ref[0]) + sh_ref[0]
        h_ref[...] = h.astype(BF16)

    @pl.when(j < n_plain)
    def _():
        o_ref[...] = _dot(h_ref[...], w_ref[...])

    @pl.when(jnp.logical_and(j >= n_plain, j < n_gelu))
    def _():
        o_ref[...] = jax.nn.gelu(_dot(h_ref[...], w_ref[...]))

    @pl.when(j >= n_gelu)
    def _():
        o_ref[...] = jax.nn.sigmoid(_dot(h_ref[...], w_ref[...]))


def _inproj(x2, modv, w_bf, rows_per_batch, mod_row0, n_cols, plain_cols, gelu_cols):
    r, d = x2.shape
    tm = _tile(rows_per_batch, 512)
    tn = _tile(_gcd_all(n_cols, plain_cols, gelu_cols), 1024)
    tiles_per_batch = rows_per_batch // tm
    modv3 = modv.reshape(modv.shape[0], 1, modv.shape[1])

    def mod_map(col):
        return lambda i, j: (mod_row0 + i // tiles_per_batch, 0, col)

    kern = functools.partial(_inproj_kernel, n_plain=plain_cols // tn, n_gelu=gelu_cols // tn)
    vmem = 2 * tm * d * 4 + tm * d * 2 + 2 * d * tn * 2 + 4 * tm * tn * 4 + (4 << 20)
    return pl.pallas_call(
        kern,
        grid=(r // tm, n_cols // tn),
        in_specs=[pl.BlockSpec((tm, d), lambda i, j: (i, 0)),
                  pl.BlockSpec((1, 1, d), mod_map(1)),
                  pl.BlockSpec((1, 1, d), mod_map(0)),
                  pl.BlockSpec((d, tn), lambda i, j: (0, j))],
        out_specs=pl.BlockSpec((tm, tn), lambda i, j: (i, j)),
        out_shape=jax.ShapeDtypeStruct((r, n_cols), F32),
        scratch_shapes=[pltpu.VMEM((tm, d), BF16)],
        compiler_params=_params(("parallel", "arbitrary"), vmem),
        name="inproj",
    )(x2, modv3, modv3, w_bf)


def _gcd_all(*vals):
    import math
    g = 0
    for v in vals:
        g = math.gcd(g, v)
    return g


def _lru_kernel(zr_ref, gy_ref, zc_ref, cw_ref, cb_ref, wax_ref, ba_ref, bx_ref, lam_ref,
                o_ref, hf_ref, hb_ref, *, seq, ctx_len, rows, cb, hd):
    n_heads = cb // hd
    groups = rows // SUBLANES
    cw = cw_ref[...]
    cbias = cb_ref[...]
    row_in_group = lax.broadcasted_iota(jnp.int32, (groups, SUBLANES, cb), 1)

    def conv_chunk(ref, r0, total):
        p0 = jnp.maximum(r0 - SUBLANES, 0)
        prev = jnp.where(r0 > 0, ref[pl.ds(pl.multiple_of(p0, SUBLANES), SUBLANES), :], 0.0)
        n0 = jnp.minimum(r0 + rows, total - SUBLANES)
        nxt = jnp.where(r0 + rows < total, ref[pl.ds(pl.multiple_of(n0, SUBLANES), SUBLANES), :], 0.0)
        cur = ref[pl.ds(r0, rows), :]
        ext = jnp.concatenate([prev, cur, nxt], axis=0)
        n_ext = rows + 2 * SUBLANES
        u = cbias + cw[1:2] * cur
        u = u + cw[0:1] * pltpu.roll(ext, 1, axis=0)[SUBLANES:SUBLANES + rows]
        u = u + cw[2:3] * pltpu.roll(ext, n_ext - 1, axis=0)[SUBLANES:SUBLANES + rows]
        u = u + cw[3:4] * pltpu.roll(ext, n_ext - 2, axis=0)[SUBLANES:SUBLANES + rows]
        return u

    def gates(u, direction):
        ub = u.astype(BF16)
        pre_a, pre_x = [], []
        for h in range(n_heads):
            g = _dot(ub[:, h * hd:(h + 1) * hd], wax_ref[direction, h])
            pre_a.append(g[:, :hd])
            pre_x.append(g[:, hd:])
        pre_a = pre_a[0] if n_heads == 1 else jnp.concatenate(pre_a, axis=1)
        pre_x = pre_x[0] if n_heads == 1 else jnp.concatenate(pre_x, axis=1)
        r = jax.nn.sigmoid(pre_a + ba_ref[direction])
        i = jax.nn.sigmoid(pre_x + bx_ref[direction])
        neg_lam = -lam_ref[direction]
        softplus = jnp.maximum(neg_lam, 0.0) + jnp.log1p(jnp.exp(-jnp.abs(neg_lam)))
        log_a = -LRU_C * r * softplus
        a = jnp.exp(log_a)
        q = jnp.tanh(-log_a) * (1.0 + a * a)
        mult = jnp.where(q > 0.0, q * lax.rsqrt(q), 0.0)
        return a, mult * i * u

    def local_scan(a, b, reverse):
        a = a.reshape(groups, SUBLANES, cb)
        b = b.reshape(groups, SUBLANES, cb)
        d = 1
        while d < SUBLANES:
            if reverse:
                shift, ok = SUBLANES - d, row_in_group < SUBLANES - d
            else:
                shift, ok = d, row_in_group >= d
            a_s = pltpu.roll(a, shift, axis=1)
            b_s = pltpu.roll(b, shift, axis=1)
            b = b + jnp.where(ok, a * b_s, 0.0)
            a = jnp.where(ok, a * a_s, a)
            d *= 2
        return a.reshape(rows, cb), b.reshape(rows, cb)

    def scan_chunk(ref, r0, total, direction, reverse, h, dst_ref):
        u = conv_chunk(ref, r0, total)
        a, b = gates(u, direction)
        a, b = local_scan(a, b, reverse)
        order = range(groups - 1, -1, -1) if reverse else range(groups)
        for g in order:
            sl = slice(g * SUBLANES, (g + 1) * SUBLANES)
            hg = b[sl] + a[sl] * h
            if dst_ref is not None:
                dst_ref[pl.ds(pl.multiple_of(r0 + g * SUBLANES, SUBLANES), SUBLANES), :] = hg
            h = hg[0:1] if reverse else hg[SUBLANES - 1:SUBLANES]
        return h

    def sweep(ref, total, h_fwd, h_bwd, fwd_dst, bwd_dst):
        n = total // rows

        def body(c, carry):
            hf, hb = carry
            hf = scan_chunk(ref, pl.multiple_of(c * rows, rows), total, 0, False, hf, fwd_dst)
            hb = scan_chunk(ref, pl.multiple_of((n - 1 - c) * rows, rows), total, 1, True, hb, bwd_dst)
            return hf, hb

        return lax.fori_loop(0, n, body, (h_fwd, h_bwd))

    zero = jnp.zeros((1, cb), F32)
    h_f, h_b = sweep(zc_ref, ctx_len, zero, zero, None, None)
    sweep(zr_ref, seq, h_f, h_b, hf_ref, hb_ref)

    def out_body(c, carry):
        sl = pl.ds(pl.multiple_of(c * rows, rows), rows)
        o_ref[sl, :] = ((hf_ref[sl, :] + hb_ref[sl, :]) * gy_ref[sl, :]).astype(BF16)
        return carry

    lax.fori_loop(0, seq // rows, out_body, 0)


def _lru(z, zc, conv_w, conv_b, wax, ba, bx, lam, batch, seq, ctx_len, w_lru, hd, gy_col0):
    cb = _tile(w_lru, 2 * hd)
    rows = _tile(_gcd_all(seq, ctx_len), 256)
    n_cb = w_lru // cb
    kern = functools.partial(_lru_kernel, seq=seq, ctx_len=ctx_len, rows=rows, cb=cb, hd=hd)
    vmem = 4 * seq * cb * 4 + 2 * seq * cb * 2 + 2 * seq * cb * 4 + 80 * rows * cb * 4 + (4 << 20)
    return pl.pallas_call(
        kern,
        grid=(batch, n_cb),
        in_specs=[pl.BlockSpec((seq, cb), lambda b, j: (b, j)),
                  pl.BlockSpec((seq, cb), lambda b, j: (b, gy_col0 // cb + j)),
                  pl.BlockSpec((ctx_len, cb), lambda b, j: (b, j)),
                  pl.BlockSpec((CONV_WIDTH, cb), lambda b, j: (0, j)),
                  pl.BlockSpec((1, cb), lambda b, j: (0, j)),
                  pl.BlockSpec((2, cb // hd, hd, 2 * hd), lambda b, j: (0, j, 0, 0)),
                  pl.BlockSpec((2, 1, cb), lambda b, j: (0, 0, j)),
                  pl.BlockSpec((2, 1, cb), lambda b, j: (0, 0, j)),
                  pl.BlockSpec((2, 1, cb), lambda b, j: (0, 0, j))],
        out_specs=pl.BlockSpec((seq, cb), lambda b, j: (b, j)),
        out_shape=jax.ShapeDtypeStruct((batch * seq, w_lru), BF16),
        scratch_shapes=[pltpu.VMEM((seq, cb), F32), pltpu.VMEM((seq, cb), F32)],
        compiler_params=_params(("parallel", "parallel"), vmem),
        name="lru",
    )(z, z, zc, conv_w, conv_b.reshape(1, w_lru), wax, ba, bx, lam)


def _sg_kernel(u_ref, v_ref, g_ref, b_ref, sw_ref, sbt_ref, o_ref, *, n_sub, chunk, n_groups, gd):
    vn = _layer_norm_rows(v_ref[...], g_ref[...], b_ref[...]).astype(BF16)
    for c in range(n_sub):
        rs = slice(c * chunk, (c + 1) * chunk)
        for g in range(n_groups):
            cs = slice(g * gd, (g + 1) * gd)
            mixed = _dot(sw_ref[g], vn[rs, cs]) + sbt_ref[:, g:g + 1]
            o_ref[rs, cs] = (u_ref[rs, cs] * mixed).astype(BF16)


def _spatial_gating(z, ln_g, ln_b, sw_bf, sb_t, n_rows, w_sg, u_col0, v_col0):
    n_groups, chunk, _ = sw_bf.shape
    gd = w_sg // n_groups
    n_sub = _tile(n_rows // chunk, 4)
    tm = n_sub * chunk
    kern = functools.partial(_sg_kernel, n_sub=n_sub, chunk=chunk, n_groups=n_groups, gd=gd)
    vmem = 4 * tm * w_sg * 4 + 2 * tm * w_sg * 2 + 6 * tm * w_sg * 4 + (4 << 20)
    return pl.pallas_call(
        kern,
        grid=(n_rows // tm,),
        in_specs=[pl.BlockSpec((tm, w_sg), lambda i: (i, u_col0 // w_sg)),
                  pl.BlockSpec((tm, w_sg), lambda i: (i, v_col0 // w_sg)),
                  pl.BlockSpec((1, w_sg), lambda i: (0, 0)),
                  pl.BlockSpec((1, w_sg), lambda i: (0, 0)),
                  pl.BlockSpec((n_groups, chunk, chunk), lambda i: (0, 0, 0)),
                  pl.BlockSpec((chunk, n_groups), lambda i: (0, 0))],
        out_specs=pl.BlockSpec((tm, w_sg), lambda i: (i, 0)),
        out_shape=jax.ShapeDtypeStruct((n_rows, w_sg), BF16),
        compiler_params=_params(("parallel",), vmem),
        name="sg",
    )(z, z, ln_g.reshape(1, w_sg), ln_b.reshape(1, w_sg), sw_bf, sb_t)


def _merge_kernel(ya_ref, yb_ref, wa_ref, wb_ref, ga_ref, gb_ref, o_ref):
    pa = _dot(ya_ref[...], wa_ref[...])
    pb = _dot(yb_ref[...], wb_ref[...])
    o_ref[...] = (ga_ref[...] * pa + gb_ref[...] * pb).astype(BF16)


def _merge(ya, yb, wa_bf, wb_bf, z, ga_col0, gb_col0):
    r, ka = ya.shape
    kb = yb.shape[1]
    d = wa_bf.shape[1]
    tm = _tile(r, 1024)
    tn = _tile(d, 512)
    vmem = 2 * (tm * (ka + kb) * 2 + (ka + kb) * tn * 2 + 2 * tm * tn * 4 + tm * tn * 2) \
        + 4 * tm * tn * 4 + (4 << 20)
    return pl.pallas_call(
        _merge_kernel,
        grid=(r // tm, d // tn),
        in_specs=[pl.BlockSpec((tm, ka), lambda i, j: (i, 0)),
                  pl.BlockSpec((tm, kb), lambda i, j: (i, 0)),
                  pl.BlockSpec((ka, tn), lambda i, j: (0, j)),
                  pl.BlockSpec((kb, tn), lambda i, j: (0, j)),
                  pl.BlockSpec((tm, tn), lambda i, j: (i, ga_col0 // tn + j)),
                  pl.BlockSpec((tm, tn), lambda i, j: (i, gb_col0 // tn + j))],
        out_specs=pl.BlockSpec((tm, tn), lambda i, j: (i, j)),
        out_shape=jax.ShapeDtypeStruct((r, d), BF16),
        compiler_params=_params(("parallel", "parallel"), vmem),
        name="merge",
    )(ya, yb, wa_bf, wb_bf, z, z)


def _outproj_kernel(m_ref, w_ref, x_ref, g_ref, o_ref):
    o_ref[...] = DN_ALPHA * x_ref[...] + g_ref[0] * _dot(m_ref[...], w_ref[...])


def _outproj(merged, wo_bf, x2, modv, rows_per_batch, g_col):
    r, k = merged.shape
    d = wo_bf.shape[1]
    tm = _tile(rows_per_batch, 1024)
    tn = _tile(d, 1024)
    tiles_per_batch = rows_per_batch // tm
    modv3 = modv.reshape(modv.shape[0], 1, modv.shape[1])
    vmem = 2 * (tm * k * 2 + k * tn * 2 + 2 * tm * tn * 4) + 2 * tm * tn * 4 + (4 << 20)
    return pl.pallas_call(
        _outproj_kernel,
        grid=(r // tm, d // tn),
        in_specs=[pl.BlockSpec((tm, k), lambda i, j: (i, 0)),
                  pl.BlockSpec((k, tn), lambda i, j: (0, j)),
                  pl.BlockSpec((tm, tn), lambda i, j: (i, j)),
                  pl.BlockSpec((1, 1, tn), lambda i, j: (i // tiles_per_batch, 0, g_col * (d // tn) + j))],
        out_specs=pl.BlockSpec((tm, tn), lambda i, j: (i, j)),
        out_shape=jax.ShapeDtypeStruct((r, d), F32),
        compiler_params=_params(("parallel", "parallel"), vmem),
        name="outproj",
    )(merged, wo_bf, x2, modv3)


def _ln_route_kernel(r_ref, g_ref, b_ref, sc_ref, sh_ref, wrt_ref, br_ref,
                     x1_ref, tok_ref, idx_ref, gate_ref, rank_ref, cnt_ref, run_ref, *, n_exp):
    i = pl.program_id(0)
    tm = r_ref.shape[0]

    @pl.when(i == 0)
    def _():
        run_ref[...] = jnp.zeros_like(run_ref)

    x1 = _layer_norm_rows(r_ref[...], g_ref[...], b_ref[...])
    x1_ref[...] = x1
    tok = x1 * (1.0 + sc_ref[0]) + sh_ref[0]
    tok_ref[...] = tok

    logits = lax.dot_general(wrt_ref[...], tok, (((1,), (1,)), ((), ())),
                             precision=lax.Precision.HIGHEST,
                             preferred_element_type=F32) + br_ref[...]
    eidx = lax.broadcasted_iota(jnp.int32, (n_exp, tm), 0)
    vals, idxs = [], []
    work = logits
    for _ in range(TOP_K):
        m = jnp.max(work, axis=0, keepdims=True)
        idx = jnp.min(jnp.where(work == m, eidx, n_exp), axis=0, keepdims=True)
        vals.append(m)
        idxs.append(idx)
        work = jnp.where(eidx == idx, -jnp.inf, work)
    exps = [jnp.exp(v - vals[0]) for v in vals]
    denom = exps[0]
    for e in exps[1:]:
        denom = denom + e
    gate_ref[...] = jnp.concatenate([e / denom for e in exps], axis=0)
    idx_ref[...] = jnp.concatenate(idxs, axis=0)

    sel = [eidx == idx for idx in idxs]
    onehot = sel[0]
    for s in sel[1:]:
        onehot = jnp.logical_or(onehot, s)
    onehot = jnp.where(onehot, 1.0, 0.0)
    s_idx = lax.broadcasted_iota(jnp.int32, (tm, tm), 0)
    t_idx = lax.broadcasted_iota(jnp.int32, (tm, tm), 1)
    strict_upper = jnp.where(s_idx < t_idx, 1.0, 0.0).astype(BF16)
    before = _dot(onehot.astype(BF16), strict_upper) + run_ref[:, 0:1]
    ranks = [jnp.sum(jnp.where(s, before, 0.0), axis=0, keepdims=True) for s in sel]
    rank_ref[...] = jnp.concatenate(ranks, axis=0).astype(jnp.int32)
    run_ref[...] = run_ref[...] + jnp.sum(onehot, axis=1, keepdims=True)
    cnt_ref[...] = run_ref[...].astype(jnp.int32)


def _ln_route(r2, ln_g, ln_b, modv, wr_t, b_r, rows_per_batch, sc_col, sh_col):
    r, d = r2.shape
    n_exp = wr_t.shape[0]
    tm = _tile(rows_per_batch, 256)
    tiles_per_batch = rows_per_batch // tm
    modv3 = modv.reshape(modv.shape[0], 1, modv.shape[1])
    kern = functools.partial(_ln_route_kernel, n_exp=n_exp)
    vmem = 6 * tm * d * 4 + 6 * tm * d * 4 + (8 << 20)
    small = lambda dt: jax.ShapeDtypeStruct((TOP_K, r), dt)
    return pl.pallas_call(
        kern,
        grid=(r // tm,),
        in_specs=[pl.BlockSpec((tm, d), lambda i: (i, 0)),
                  pl.BlockSpec((1, d), lambda i: (0, 0)),
                  pl.BlockSpec((1, d), lambda i: (0, 0)),
                  pl.BlockSpec((1, 1, d), lambda i: (i // tiles_per_batch, 0, sc_col)),
                  pl.BlockSpec((1, 1, d), lambda i: (i // tiles_per_batch, 0, sh_col)),
                  pl.BlockSpec((n_exp, d), lambda i: (0, 0)),
                  pl.BlockSpec((n_exp, 1), lambda i: (0, 0))],
        out_specs=[pl.BlockSpec((tm, d), lambda i: (i, 0)),
                   pl.BlockSpec((tm, d), lambda i: (i, 0)),
                   pl.BlockSpec((TOP_K, tm), lambda i: (0, i)),
                   pl.BlockSpec((TOP_K, tm), lambda i: (0, i)),
                   pl.BlockSpec((TOP_K, tm), lambda i: (0, i)),
                   pl.BlockSpec((n_exp, LANES), lambda i: (0, 0))],
        out_shape=[jax.ShapeDtypeStruct((r, d), F32),
                   jax.ShapeDtypeStruct((r, d), F32),
                   small(jnp.int32), small(F32), small(jnp.int32),
                   jax.ShapeDtypeStruct((n_exp, LANES), jnp.int32)],
        scratch_shapes=[pltpu.VMEM((n_exp, LANES), F32)],
        compiler_params=_params(("arbitrary",), vmem),
        name="ln_route",
    )(r2, ln_g.reshape(1, d), ln_b.reshape(1, d), modv3, modv3, wr_t, b_r.reshape(n_exp, 1))


def _deinterleave_kernel(w_ref, o_ref, *, f, grp):
    half = grp // 2
    s_idx = lax.broadcasted_iota(jnp.int32, (grp, grp), 0)
    t_idx = lax.broadcasted_iota(jnp.int32, (grp, grp), 1)
    src = jnp.where(t_idx < half, 2 * t_idx, 2 * (t_idx - half) + 1)
    perm = jnp.where(s_idx == src, 1.0, 0.0).astype(BF16)
    for g in range(2 * f // grp):
        p = _dot(w_ref[0, :, g * grp:(g + 1) * grp].astype(BF16), perm)
        o_ref[0, :, g * half:(g + 1) * half] = p[:, :half].astype(BF16)
        o_ref[0, :, f + g * half:f + (g + 1) * half] = p[:, half:].astype(BF16)


def _deinterleave_gate_up(w_gu):
    n_exp, d, f2 = w_gu.shape
    tr = _tile(d, 1024)
    grp = min(f2, 2 * LANES)
    kern = functools.partial(_deinterleave_kernel, f=f2 // 2, grp=grp)
    vmem = 2 * (tr * f2 * 4 + tr * f2 * 2) + 4 * tr * grp * 4 + (4 << 20)
    return pl.pallas_call(
        kern,
        grid=(n_exp, d // tr),
        in_specs=[pl.BlockSpec((1, tr, f2), lambda e, i: (e, i, 0))],
        out_specs=pl.BlockSpec((1, tr, f2), lambda e, i: (e, i, 0)),
        out_shape=jax.ShapeDtypeStruct((n_exp, d, f2), BF16),
        compiler_params=_params(("parallel", "parallel"), vmem),
        name="wgu_prep",
    )(w_gu)


def _start_row_gather(idx_ref, n_rows, src_ref, dst_ref, sem, unrolled):
    def start(r):
        pltpu.make_async_copy(src_ref.at[pl.ds(idx_ref[0, 0, r], 1)], dst_ref.at[pl.ds(r, 1)], sem).start()

    if unrolled:
        for r in range(n_rows):
            start(r)
    else:
        def body(r, carry):
            start(r)
            return carry
        lax.fori_loop(0, n_rows, body, 0, unroll=8)


def _wait_row_gather(dst_ref, sem):
    pltpu.make_async_copy(dst_ref, dst_ref, sem).wait()


def _moe_up_kernel(be_ref, nu_ref, idx_cur_ref, idx_nxt_ref, tok_ref, w_ref, bg_ref, bl_ref,
                   h_ref, xbuf0, xbuf1, sems, *, tm, n_tiles, f):
    i = pl.program_id(0)
    bufs = (xbuf0, xbuf1)

    @pl.when(i == 0)
    def _():
        _start_row_gather(idx_cur_ref, tm, tok_ref, xbuf0, sems.at[0], False)

    def step(cur):
        nxt = 1 - cur
        _wait_row_gather(bufs[cur], sems.at[cur])

        @pl.when(i < nu_ref[0])
        def _():
            _start_row_gather(idx_nxt_ref, tm, tok_ref, bufs[nxt], sems.at[nxt], True)
            x = bufs[cur][...].astype(BF16)
            glu = jnp.minimum(_dot(x, w_ref[0, :, :f]) + bg_ref[0], SWIGLU_LIMIT)
            lin = jnp.clip(_dot(x, w_ref[0, :, f:]) + bl_ref[0], -SWIGLU_LIMIT, SWIGLU_LIMIT)
            h_ref[...] = (glu * jax.nn.sigmoid(SWIGLU_ALPHA * glu) * (lin + 1.0)).astype(BF16)

        @pl.when(i >= nu_ref[0])
        def _():
            _start_row_gather(idx_nxt_ref, tm, tok_ref, bufs[nxt], sems.at[nxt], False)
            h_ref[...] = jnp.zeros_like(h_ref)

    for parity in range(2):
        pl.when(i % 2 == parity)(functools.partial(step, parity))

    @pl.when(i == n_tiles - 1)
    def _():
        last_nxt = 1 - (n_tiles - 1) % 2
        _wait_row_gather(bufs[last_nxt], sems.at[last_nxt])


def _moe_up(tokens, row_idx, blk_exp, n_used, wgu_bf, bg, bl, tm):
    n = row_idx.shape[0]
    d = tokens.shape[1]
    n_exp, _, f2 = wgu_bf.shape
    f = f2 // 2
    n_tiles = n // tm
    idx3 = row_idx.reshape(n_tiles, 1, tm)
    kern = functools.partial(_moe_up_kernel, tm=tm, n_tiles=n_tiles, f=f)
    vmem = 2 * tm * d * 4 + 2 * d * f2 * 2 + 2 * tm * f * 2 + 6 * tm * f * 4 + tm * d * 2 + (4 << 20)
    grid_spec = pltpu.PrefetchScalarGridSpec(
        num_scalar_prefetch=2,
        grid=(n_tiles,),
        in_specs=[pl.BlockSpec((1, 1, tm), lambda i, be, nu: (i, 0, 0), memory_space=pltpu.SMEM),
                  pl.BlockSpec((1, 1, tm), lambda i, be, nu: (jnp.minimum(i + 1, n_tiles - 1), 0, 0),
                               memory_space=pltpu.SMEM),
                  pl.BlockSpec(memory_space=pl.ANY),
                  pl.BlockSpec((1, d, f2), lambda i, be, nu: (be[i], 0, 0)),
                  pl.BlockSpec((1, 1, f), lambda i, be, nu: (be[i], 0, 0)),
                  pl.BlockSpec((1, 1, f), lambda i, be, nu: (be[i], 0, 0))],
        out_specs=pl.BlockSpec((tm, f), lambda i, be, nu: (i, 0)),
        scratch_shapes=[pltpu.VMEM((tm, d), tokens.dtype), pltpu.VMEM((tm, d), tokens.dtype),
                        pltpu.SemaphoreType.DMA((2,))])
    return pl.pallas_call(
        kern,
        grid_spec=grid_spec,
        out_shape=jax.ShapeDtypeStruct((n, f), BF16),
        compiler_params=_params(("arbitrary",), vmem),
        name="moe_up",
    )(blk_exp, n_used, idx3, idx3, tokens, wgu_bf, bg.reshape(n_exp, 1, f), bl.reshape(n_exp, 1, f))


def _moe_down_kernel(be_ref, nu_ref, h_ref, w_ref, b_ref, y_ref):
    i = pl.program_id(0)

    @pl.when(i < nu_ref[0])
    def _():
        y_ref[...] = _dot(h_ref[...], w_ref[0]) + b_ref[0]

    @pl.when(i >= nu_ref[0])
    def _():
        y_ref[...] = jnp.zeros_like(y_ref)


def _moe_down(hs, blk_exp, n_used, wd_bf, bd, tm):
    n, f = hs.shape
    n_exp, _, d = wd_bf.shape
    vmem = 2 * (tm * f * 2 + f * d * 2 + tm * d * 4) + 2 * tm * d * 4 + (4 << 20)
    grid_spec = pltpu.PrefetchScalarGridSpec(
        num_scalar_prefetch=2,
        grid=(n // tm,),
        in_specs=[pl.BlockSpec((tm, f), lambda i, be, nu: (i, 0)),
                  pl.BlockSpec((1, f, d), lambda i, be, nu: (be[i], 0, 0)),
                  pl.BlockSpec((1, 1, d), lambda i, be, nu: (be[i], 0, 0))],
        out_specs=pl.BlockSpec((tm, d), lambda i, be, nu: (i, 0)))
    return pl.pallas_call(
        _moe_down_kernel,
        grid_spec=grid_spec,
        out_shape=jax.ShapeDtypeStruct((n, d), F32),
        compiler_params=_params(("arbitrary",), vmem),
        name="moe_down",
    )(blk_exp, n_used, hs, wd_bf, bd.reshape(n_exp, 1, d))


def _combine_kernel(pos_cur_ref, pos_nxt_ref, y_ref, gate_ref, x1_ref, g2_ref, lg_ref, lb_ref,
                    o_ref, buf0, buf1, sems, *, tm, n_steps):
    i = pl.program_id(0)
    bufs = (buf0, buf1)
    n_rows = TOP_K * tm

    @pl.when(i == 0)
    def _():
        _start_row_gather(pos_cur_ref, n_rows, y_ref, buf0, sems.at[0], False)

    def step(cur):
        nxt = 1 - cur
        _wait_row_gather(bufs[cur], sems.at[cur])
        _start_row_gather(pos_nxt_ref, n_rows, y_ref, bufs[nxt], sems.at[nxt], True)
        gates = gate_ref[...]
        moe = gates[:, 0:1] * bufs[cur][0:tm, :]
        for k in range(1, TOP_K):
            moe = moe + gates[:, k:k + 1] * bufs[cur][k * tm:(k + 1) * tm, :]
        pre = DN_ALPHA * x1_ref[...] + g2_ref[0] * moe
        o_ref[...] = _layer_norm_rows(pre, lg_ref[...], lb_ref[...])

    for parity in range(2):
        pl.when(i % 2 == parity)(functools.partial(step, parity))

    @pl.when(i == n_steps - 1)
    def _():
        last_nxt = 1 - (n_steps - 1) % 2
        _wait_row_gather(bufs[last_nxt], sems.at[last_nxt])


def _combine(y, pos, gates_t, x1, modv, ln_g, ln_b, rows_per_batch, g_col):
    r, d = x1.shape
    tm = _tile(rows_per_batch, 128)
    tiles_per_batch = rows_per_batch // tm
    n_steps = r // tm
    modv3 = modv.reshape(modv.shape[0], 1, modv.shape[1])
    pos_steps = pos.reshape(TOP_K, n_steps, tm).transpose(1, 0, 2).reshape(n_steps, 1, TOP_K * tm)
    kern = functools.partial(_combine_kernel, tm=tm, n_steps=n_steps)
    vmem = 2 * TOP_K * tm * d * 4 + 4 * tm * d * 4 + 4 * tm * d * 4 + (4 << 20)
    return pl.pallas_call(
        kern,
        grid=(n_steps,),
        in_specs=[pl.BlockSpec((1, 1, TOP_K * tm), lambda i: (i, 0, 0), memory_space=pltpu.SMEM),
                  pl.BlockSpec((1, 1, TOP_K * tm), lambda i: (jnp.minimum(i + 1, n_steps - 1), 0, 0),
                               memory_space=pltpu.SMEM),
                  pl.BlockSpec(memory_space=pl.ANY),
                  pl.BlockSpec((tm, TOP_K), lambda i: (i, 0)),
                  pl.BlockSpec((tm, d), lambda i: (i, 0)),
                  pl.BlockSpec((1, 1, d), lambda i: (i // tiles_per_batch, 0, g_col)),
                  pl.BlockSpec((1, d), lambda i: (0, 0)),
                  pl.BlockSpec((1, d), lambda i: (0, 0))],
        out_specs=pl.BlockSpec((tm, d), lambda i: (i, 0)),
        out_shape=jax.ShapeDtypeStruct((r, d), F32),
        scratch_shapes=[pltpu.VMEM((TOP_K * tm, d), F32), pltpu.VMEM((TOP_K * tm, d), F32),
                        pltpu.SemaphoreType.DMA((2,))],
        compiler_params=_params(("arbitrary",), vmem),
        name="combine",
    )(pos_steps, pos_steps, y, gates_t, x1, modv3, ln_g.reshape(1, d), ln_b.reshape(1, d))


def kernel(x, c, ctx, c_ctx, w_mod, b_mod, w_in, conv_w, conv_b, lru_wa, lru_ba, lru_wx, lru_bx,
           lru_lambda, sg_ln_g, sg_ln_b, sg_w, sg_b, w_branch_a, w_branch_b, w_out, ln1_g, ln1_b,
           w_router, b_router, w_gate_up, b_gate_up, w_down, b_down, ln2_g, ln2_b):
    assert w_mod.shape[0] == DEPTH
    batch, seq, d = x.shape
    ctx_len = ctx.shape[1]
    w_lru = conv_w.shape[-1]
    n_heads = lru_wa.shape[2]
    hd = w_lru // n_heads
    w_sg = sg_ln_g.shape[-1]
    n_exp = w_router.shape[-1]
    n_tok = batch * seq
    l = 0

    pad = (-(batch + 1)) % SUBLANES
    c_rows = jnp.concatenate([c, c_ctx[None, :], jnp.zeros((pad, d), F32)], axis=0)
    modv = _mod(c_rows, w_mod[l], b_mod[l])

    w_in_bf = w_in[l].astype(BF16)
    n_in = w_in_bf.shape[1]
    col_gy, col_u, col_v = w_lru, 2 * w_lru, 2 * w_lru + w_sg
    col_ga = 2 * w_lru + 2 * w_sg
    col_gb = col_ga + d
    x2 = x.reshape(n_tok, d)
    z = _inproj(x2, modv, w_in_bf, seq, 0, n_in, col_gy, col_ga)
    zc = _inproj(ctx.reshape(batch * ctx_len, d), modv, w_in_bf, batch * ctx_len, batch,
                 w_lru, w_lru, w_lru)

    wax = jnp.concatenate([lru_wa[l], lru_wx[l]], axis=-1).astype(BF16)
    y_a = _lru(z, zc, conv_w[l], conv_b[l], wax, lru_ba[l].reshape(2, 1, w_lru),
               lru_bx[l].reshape(2, 1, w_lru), lru_lambda[l].reshape(2, 1, w_lru),
               batch, seq, ctx_len, w_lru, hd, col_gy)
    y_b = _spatial_gating(z, sg_ln_g[l], sg_ln_b[l], sg_w[l].astype(BF16), sg_b[l].T,
                          n_tok, w_sg, col_u, col_v)

    merged = _merge(y_a, y_b, w_branch_a[l].astype(BF16), w_branch_b[l].astype(BF16), z, col_ga, col_gb)
    r1 = _outproj(merged, w_out[l].astype(BF16), x2, modv, seq, 2)
    x1, tokens, idx_t, gate_t, rank_t, counts = _ln_route(
        r1, ln1_g[l], ln1_b[l], modv, w_router[l].T, b_router[l], seq, 4, 3)

    tm_moe = 256
    n_assign = n_tok * TOP_K
    n_tiles = -(-(n_assign + n_exp * (tm_moe - 1)) // tm_moe)
    n_pad = n_tiles * tm_moe
    cnt = counts[:, 0]
    padded = (cnt + tm_moe - 1) // tm_moe * tm_moe
    pad_end = jnp.cumsum(padded)
    pad_start = pad_end - padded
    expert_ids = jnp.arange(n_exp, dtype=jnp.int32)
    pos = rank_t + jnp.sum(jnp.where(idx_t[..., None] == expert_ids, pad_start, 0), axis=-1)
    tok_ids = jnp.broadcast_to(jnp.arange(n_tok, dtype=jnp.int32)[None, :], (TOP_K, n_tok))
    tok_pad = jnp.zeros((n_pad,), jnp.int32).at[pos.reshape(-1)].set(tok_ids.reshape(-1))
    blk_start = jnp.arange(n_tiles, dtype=jnp.int32) * tm_moe
    blk_exp = jnp.minimum(jnp.sum(blk_start[:, None] >= pad_end[None, :], axis=1), n_exp - 1).astype(jnp.int32)
    n_used = (pad_end[-1] // tm_moe).astype(jnp.int32).reshape(1)

    wgu = _deinterleave_gate_up(w_gate_up[l])
    hs = _moe_up(tokens, tok_pad, blk_exp, n_used, wgu,
                 b_gate_up[l][..., ::2], b_gate_up[l][..., 1::2], tm_moe)
    ys = _moe_down(hs, blk_exp, n_used, w_down[l].astype(BF16), b_down[l], tm_moe)

    out = _combine(ys, pos, gate_t.T, x1, modv, ln2_g[l], ln2_b[l], seq, 5)
    return out.reshape(batch, seq, d)
```

```python
import functools

import jax
import jax.numpy as jnp
from jax import lax
from jax.experimental import pallas as pl
from jax.experimental.pallas import tpu as pltpu

TOP_K = 4
LRU_C = 8.0
CONV_WIDTH = 4
CONV_PAD_LEFT = 1
SWIGLU_LIMIT = 7.0
SWIGLU_ALPHA = 1.702
DEPTH = 1
DN_ALPHA = (2.0 * DEPTH) ** 0.25
LN_EPS = 1e-5
N_MOD = 6

SUBLANES = 8
LANES = 128
V7X_VMEM_CAP = 60000 * 1024

F32 = jnp.float32
BF16 = jnp.bfloat16


def _params(semantics, vmem_bytes):
    return pltpu.CompilerParams(dimension_semantics=semantics,
                                vmem_limit_bytes=int(min(vmem_bytes, V7X_VMEM_CAP)))


def _tile(n, pref):
    t = min(n, pref)
    while n % t:
        t //= 2
    return t


def _dot(a, b):
    return jnp.dot(a, b, preferred_element_type=F32)


def _layer_norm_rows(x, g, b):
    mu = jnp.mean(x, axis=-1, keepdims=True)
    xc = x - mu
    var = jnp.mean(xc * xc, axis=-1, keepdims=True)
    return xc * lax.rsqrt(var + LN_EPS) * g + b


def _mod_kernel(c_ref, w_ref, b_ref, o_ref):
    c = c_ref[...]
    s = c * jax.nn.sigmoid(c)
    o_ref[...] = _dot(s.astype(BF16), w_ref[...].astype(BF16)) + b_ref[...]


def _mod(c_rows, w_mod, b_mod):
    m, d = c_rows.shape
    n = w_mod.shape[1]
    tn = _tile(n, 512)
    return pl.pallas_call(
        _mod_kernel,
        grid=(n // tn,),
        in_specs=[pl.BlockSpec((m, d), lambda j: (0, 0)),
                  pl.BlockSpec((d, tn), lambda j: (0, j)),
                  pl.BlockSpec((1, tn), lambda j: (0, j))],
        out_specs=pl.BlockSpec((m, tn), lambda j: (0, j)),
        out_shape=jax.ShapeDtypeStruct((m, n), F32),
        compiler_params=_params(("arbitrary",), 3 * d * tn * 4 + (4 << 20)),
        name="mod",
    )(c_rows, w_mod, b_mod.reshape(1, n))


def _inproj_kernel(x_ref, sc_ref, sh_ref, w_ref, o_ref, h_ref, *, n_plain, n_gelu):
    j = pl.program_id(1)

    @pl.when(j == 0)
    def _():
        h = x_ref[...] * (1.0 + sc_ref[0]) + sh_ref[0]
        h_ref[...] = h.astype(BF16)

    @pl.when(j < n_plain)
    def _():
        o_ref[...] = _dot(h_ref[...], w_ref[...])

    @pl.when(jnp.logical_and(j >= n_plain, j < n_gelu))
    def _():
        o_ref[...] = jax.nn.gelu(_dot(h_ref[...], w_ref[...]))

    @pl.when(j >= n_gelu)
    def _():
        o_ref[...] = jax.nn.sigmoid(_dot(h_ref[...], w_ref[...]))


def _inproj(x2, modv, w_bf, rows_per_batch, mod_row0, n_cols, plain_cols, gelu_cols):
    r, d = x2.shape
    tm = _tile(rows_per_batch, 512)
    tn = _tile(_gcd_all(n_cols, plain_cols, gelu_cols), 1024)
    tiles_per_batch = rows_per_batch // tm
    modv3 = modv.reshape(modv.shape[0], 1, modv.shape[1])

    def mod_map(col):
        return lambda i, j: (mod_row0 + i // tiles_per_batch, 0, col)

    kern = functools.partial(_inproj_kernel, n_plain=plain_cols // tn, n_gelu=gelu_cols // tn)
    vmem = 2 * tm * d * 4 + tm * d * 2 + 2 * d * tn * 2 + 4 * tm * tn * 4 + (4 << 20)
    return pl.pallas_call(
        kern,
        grid=(r // tm, n_cols // tn),
        in_specs=[pl.BlockSpec((tm, d), lambda i, j: (i, 0)),
                  pl.BlockSpec((1, 1, d), mod_map(1)),
                  pl.BlockSpec((1, 1, d), mod_map(0)),
                  pl.BlockSpec((d, tn), lambda i, j: (0, j))],
        out_specs=pl.BlockSpec((tm, tn), lambda i, j: (i, j)),
        out_shape=jax.ShapeDtypeStruct((r, n_cols), F32),
        scratch_shapes=[pltpu.VMEM((tm, d), BF16)],
        compiler_params=_params(("parallel", "arbitrary"), vmem),
        name="inproj",
    )(x2, modv3, modv3, w_bf)


def _gcd_all(*vals):
    import math
    g = 0
    for v in vals:
        g = math.gcd(g, v)
    return g


def _lru_kernel(zr_ref, gy_ref, zc_ref, cw_ref, cb_ref, wax_ref, ba_ref, bx_ref, lam_ref,
                o_ref, hf_ref, hb_ref, *, seq, ctx_len, rows, cb, hd):
    n_heads = cb // hd
    groups = rows // SUBLANES
    cw = cw_ref[...]
    cbias = cb_ref[...]
    row_in_group = lax.broadcasted_iota(jnp.int32, (groups, SUBLANES, cb), 1)

    def conv_chunk(ref, r0, total):
        p0 = jnp.maximum(r0 - SUBLANES, 0)
        prev = jnp.where(r0 > 0, ref[pl.ds(pl.multiple_of(p0, SUBLANES), SUBLANES), :], 0.0)
        n0 = jnp.minimum(r0 + rows, total - SUBLANES)
        nxt = jnp.where(r0 + rows < total, ref[pl.ds(pl.multiple_of(n0, SUBLANES), SUBLANES), :], 0.0)
        cur = ref[pl.ds(r0, rows), :]
        ext = jnp.concatenate([prev, cur, nxt], axis=0)
        n_ext = rows + 2 * SUBLANES
        u = cbias + cw[1:2] * cur
        u = u + cw[0:1] * pltpu.roll(ext, 1, axis=0)[SUBLANES:SUBLANES + rows]
        u = u + cw[2:3] * pltpu.roll(ext, n_ext - 1, axis=0)[SUBLANES:SUBLANES + rows]
        u = u + cw[3:4] * pltpu.roll(ext, n_ext - 2, axis=0)[SUBLANES:SUBLANES + rows]
        return u

    def gates(u, direction):
        ub = u.astype(BF16)
        pre_a, pre_x = [], []
        for h in range(n_heads):
            g = _dot(ub[:, h * hd:(h + 1) * hd], wax_ref[direction, h])
            pre_a.append(g[:, :hd])
            pre_x.append(g[:, hd:])
        pre_a = pre_a[0] if n_heads == 1 else jnp.concatenate(pre_a, axis=1)
        pre_x = pre_x[0] if n_heads == 1 else jnp.concatenate(pre_x, axis=1)
        r = jax.nn.sigmoid(pre_a + ba_ref[direction])
        i = jax.nn.sigmoid(pre_x + bx_ref[direction])
        neg_lam = -lam_ref[direction]
        softplus = jnp.maximum(neg_lam, 0.0) + jnp.log1p(jnp.exp(-jnp.abs(neg_lam)))
        log_a = -LRU_C * r * softplus
        a = jnp.exp(log_a)
        q = jnp.tanh(-log_a) * (1.0 + a * a)
        mult = jnp.where(q > 0.0, q * lax.rsqrt(q), 0.0)
        return a, mult * i * u

    def local_scan(a, b, reverse):
        a = a.reshape(groups, SUBLANES, cb)
        b = b.reshape(groups, SUBLANES, cb)
        d = 1
        while d < SUBLANES:
            if reverse:
                shift, ok = SUBLANES - d, row_in_group < SUBLANES - d
            else:
                shift, ok = d, row_in_group >= d
            a_s = pltpu.roll(a, shift, axis=1)
            b_s = pltpu.roll(b, shift, axis=1)
            b = b + jnp.where(ok, a * b_s, 0.0)
            a = jnp.where(ok, a * a_s, a)
            d *= 2
        return a.reshape(rows, cb), b.reshape(rows, cb)

    def scan_chunk(ref, r0, total, direction, reverse, h, dst_ref):
        u = conv_chunk(ref, r0, total)
        a, b = gates(u, direction)
        a, b = local_scan(a, b, reverse)
        order = range(groups - 1, -1, -1) if reverse else range(groups)
        for g in order:
            sl = slice(g * SUBLANES, (g + 1) * SUBLANES)
            hg = b[sl] + a[sl] * h
            if dst_ref is not None:
                dst_ref[pl.ds(pl.multiple_of(r0 + g * SUBLANES, SUBLANES), SUBLANES), :] = hg
            h = hg[0:1] if reverse else hg[SUBLANES - 1:SUBLANES]
        return h

    def sweep(ref, total, h_fwd, h_bwd, fwd_dst, bwd_dst):
        n = total // rows

        def body(c, carry):
            hf, hb = carry
            hf = scan_chunk(ref, pl.multiple_of(c * rows, rows), total, 0, False, hf, fwd_dst)
            hb = scan_chunk(ref, pl.multiple_of((n - 1 - c) * rows, rows), total, 1, True, hb, bwd_dst)
            return hf, hb

        return lax.fori_loop(0, n, body, (h_fwd, h_bwd))

    zero = jnp.zeros((1, cb), F32)
    h_f, h_b = sweep(zc_ref, ctx_len, zero, zero, None, None)
    sweep(zr_ref, seq, h_f, h_b, hf_ref, hb_ref)

    def out_body(c, carry):
        sl = pl.ds(pl.multiple_of(c * rows, rows), rows)
        o_ref[sl, :] = ((hf_ref[sl, :] + hb_ref[sl, :]) * gy_ref[sl, :]).astype(BF16)
        return carry

    lax.fori_loop(0, seq // rows, out_body, 0)


def _lru(z, zc, conv_w, conv_b, wax, ba, bx, lam, batch, seq, ctx_len, w_lru, hd, gy_col0):
    cb = _tile(w_lru, 2 * hd)
    rows = _tile(_gcd_all(seq, ctx_len), 256)
    n_cb = w_lru // cb
    kern = functools.partial(_lru_kernel, seq=seq, ctx_len=ctx_len, rows=rows, cb=cb, hd=hd)
    vmem = 4 * seq * cb * 4 + 2 * seq * cb * 2 + 2 * seq * cb * 4 + 80 * rows * cb * 4 + (4 << 20)
    return pl.pallas_call(
        kern,
        grid=(batch, n_cb),
        in_specs=[pl.BlockSpec((seq, cb), lambda b, j: (b, j)),
                  pl.BlockSpec((seq, cb), lambda b, j: (b, gy_col0 // cb + j)),
                  pl.BlockSpec((ctx_len, cb), lambda b, j: (b, j)),
                  pl.BlockSpec((CONV_WIDTH, cb), lambda b, j: (0, j)),
                  pl.BlockSpec((1, cb), lambda b, j: (0, j)),
                  pl.BlockSpec((2, cb // hd, hd, 2 * hd), lambda b, j: (0, j, 0, 0)),
                  pl.BlockSpec((2, 1, cb), lambda b, j: (0, 0, j)),
                  pl.BlockSpec((2, 1, cb), lambda b, j: (0, 0, j)),
                  pl.BlockSpec((2, 1, cb), lambda b, j: (0, 0, j))],
        out_specs=pl.BlockSpec((seq, cb), lambda b, j: (b, j)),
        out_shape=jax.ShapeDtypeStruct((batch * seq, w_lru), BF16),
        scratch_shapes=[pltpu.VMEM((seq, cb), F32), pltpu.VMEM((seq, cb), F32)],
        compiler_params=_params(("parallel", "parallel"), vmem),
        name="lru",
    )(z, z, zc, conv_w, conv_b.reshape(1, w_lru), wax, ba, bx, lam)


def _sg_kernel(u_ref, v_ref, g_ref, b_ref, sw_ref, sbt_ref, o_ref, *, n_sub, chunk, n_groups, gd):
    vn = _layer_norm_rows(v_ref[...], g_ref[...], b_ref[...]).astype(BF16)
    for c in range(n_sub):
        rs = slice(c * chunk, (c + 1) * chunk)
        for g in range(n_groups):
            cs = slice(g * gd, (g + 1) * gd)
            mixed = _dot(sw_ref[g], vn[rs, cs]) + sbt_ref[:, g:g + 1]
            o_ref[rs, cs] = (u_ref[rs, cs] * mixed).astype(BF16)


def _spatial_gating(z, ln_g, ln_b, sw_bf, sb_t, n_rows, w_sg, u_col0, v_col0):
    n_groups, chunk, _ = sw_bf.shape
    gd = w_sg // n_groups
    n_sub = _tile(n_rows // chunk, 4)
    tm = n_sub * chunk
    kern = functools.partial(_sg_kernel, n_sub=n_sub, chunk=chunk, n_groups=n_groups, gd=gd)
    vmem = 4 * tm * w_sg * 4 + 2 * tm * w_sg * 2 + 6 * tm * w_sg * 4 + (4 << 20)
    return pl.pallas_call(
        kern,
        grid=(n_rows // tm,),
        in_specs=[pl.BlockSpec((tm, w_sg), lambda i: (i, u_col0 // w_sg)),
                  pl.BlockSpec((tm, w_sg), lambda i: (i, v_col0 // w_sg)),
                  pl.BlockSpec((1, w_sg), lambda i: (0, 0)),
                  pl.BlockSpec((1, w_sg), lambda i: (0, 0)),
                  pl.BlockSpec((n_groups, chunk, chunk), lambda i: (0, 0, 0)),
                  pl.BlockSpec((chunk, n_groups), lambda i: (0, 0))],
        out_specs=pl.BlockSpec((tm, w_sg), lambda i: (i, 0)),
        out_shape=jax.ShapeDtypeStruct((n_rows, w_sg), BF16),
        compiler_params=_params(("parallel",), vmem),
        name="sg",
    )(z, z, ln_g.reshape(1, w_sg), ln_b.reshape(1, w_sg), sw_bf, sb_t)


def _merge_kernel(ya_ref, yb_ref, wa_ref, wb_ref, ga_ref, gb_ref, o_ref):
    pa = _dot(ya_ref[...], wa_ref[...])
    pb = _dot(yb_ref[...], wb_ref[...])
    o_ref[...] = (ga_ref[...] * pa + gb_ref[...] * pb).astype(BF16)


def _merge(ya, yb, wa_bf, wb_bf, z, ga_col0, gb_col0):
    r, ka = ya.shape
    kb = yb.shape[1]
    d = wa_bf.shape[1]
    tm = _tile(r, 1024)
    tn = _tile(d, 512)
    vmem = 2 * (tm * (ka + kb) * 2 + (ka + kb) * tn * 2 + 2 * tm * tn * 4 + tm * tn * 2) \
        + 4 * tm * tn * 4 + (4 << 20)
    return pl.pallas_call(
        _merge_kernel,
        grid=(r // tm, d // tn),
        in_specs=[pl.BlockSpec((tm, ka), lambda i, j: (i, 0)),
                  pl.BlockSpec((tm, kb), lambda i, j: (i, 0)),
                  pl.BlockSpec((ka, tn), lambda i, j: (0, j)),
                  pl.BlockSpec((kb, tn), lambda i, j: (0, j)),
                  pl.BlockSpec((tm, tn), lambda i, j: (i, ga_col0 // tn + j)),
                  pl.BlockSpec((tm, tn), lambda i, j: (i, gb_col0 // tn + j))],
        out_specs=pl.BlockSpec((tm, tn), lambda i, j: (i, j)),
        out_shape=jax.ShapeDtypeStruct((r, d), BF16),
        compiler_params=_params(("parallel", "parallel"), vmem),
        name="merge",
    )(ya, yb, wa_bf, wb_bf, z, z)


def _outproj_kernel(m_ref, w_ref, x_ref, g_ref, o_ref):
    o_ref[...] = DN_ALPHA * x_ref[...] + g_ref[0] * _dot(m_ref[...], w_ref[...])


def _outproj(merged, wo_bf, x2, modv, rows_per_batch, g_col):
    r, k = merged.shape
    d = wo_bf.shape[1]
    tm = _tile(rows_per_batch, 1024)
    tn = _tile(d, 1024)
    tiles_per_batch = rows_per_batch // tm
    modv3 = modv.reshape(modv.shape[0], 1, modv.shape[1])
    vmem = 2 * (tm * k * 2 + k * tn * 2 + 2 * tm * tn * 4) + 2 * tm * tn * 4 + (4 << 20)
    return pl.pallas_call(
        _outproj_kernel,
        grid=(r // tm, d // tn),
        in_specs=[pl.BlockSpec((tm, k), lambda i, j: (i, 0)),
                  pl.BlockSpec((k, tn), lambda i, j: (0, j)),
                  pl.BlockSpec((tm, tn), lambda i, j: (i, j)),
                  pl.BlockSpec((1, 1, tn), lambda i, j: (i // tiles_per_batch, 0, g_col * (d // tn) + j))],
        out_specs=pl.BlockSpec((tm, tn), lambda i, j: (i, j)),
        out_shape=jax.ShapeDtypeStruct((r, d), F32),
        compiler_params=_params(("parallel", "parallel"), vmem),
        name="outproj",
    )(merged, wo_bf, x2, modv3)


def _ln_route_kernel(r_ref, g_ref, b_ref, sc_ref, sh_ref, wrt_ref, br_ref,
                     x1_ref, tok_ref, idx_ref, gate_ref, rank_ref, cnt_ref, run_ref, *, n_exp):
    i = pl.program_id(0)
    tm = r_ref.shape[0]

    @pl.when(i == 0)
    def _():
        run_ref[...] = jnp.zeros_like(run_ref)

    x1 = _layer_norm_rows(r_ref[...], g_ref[...], b_ref[...])
    x1_ref[...] = x1
    tok = x1 * (1.0 + sc_ref[0]) + sh_ref[0]
    tok_ref[...] = tok

    logits = lax.dot_general(wrt_ref[...], tok, (((1,), (1,)), ((), ())),
                             precision=lax.Precision.HIGHEST,
                             preferred_element_type=F32) + br_ref[...]
    eidx = lax.broadcasted_iota(jnp.int32, (n_exp, tm), 0)
    vals, idxs = [], []
    work = logits
    for _ in range(TOP_K):
        m = jnp.max(work, axis=0, keepdims=True)
        idx = jnp.min(jnp.where(work == m, eidx, n_exp), axis=0, keepdims=True)
        vals.append(m)
        idxs.append(idx)
        work = jnp.where(eidx == idx, -jnp.inf, work)
    exps = [jnp.exp(v - vals[0]) for v in vals]
    denom = exps[0]
    for e in exps[1:]:
        denom = denom + e
    gate_ref[...] = jnp.concatenate([e / denom for e in exps], axis=0)
    idx_ref[...] = jnp.concatenate(idxs, axis=0)

    sel = [eidx == idx for idx in idxs]
    onehot = sel[0]
    for s in sel[1:]:
        onehot = jnp.logical_or(onehot, s)
    onehot = jnp.where(onehot, 1.0, 0.0)
    s_idx = lax.broadcasted_iota(jnp.int32, (tm, tm), 0)
    t_idx = lax.broadcasted_iota(jnp.int32, (tm, tm), 1)
    strict_upper = jnp.where(s_idx < t_idx, 1.0, 0.0).astype(BF16)
    before = _dot(onehot.astype(BF16), strict_upper) + run_ref[:, 0:1]
    ranks = [jnp.sum(jnp.where(s, before, 0.0), axis=0, keepdims=True) for s in sel]
    rank_ref[...] = jnp.concatenate(ranks, axis=0).astype(jnp.int32)
    run_ref[...] = run_ref[...] + jnp.sum(onehot, axis=1, keepdims=True)
    cnt_ref[...] = run_ref[...].astype(jnp.int32)


def _ln_route(r2, ln_g, ln_b, modv, wr_t, b_r, rows_per_batch, sc_col, sh_col):
    r, d = r2.shape
    n_exp = wr_t.shape[0]
    tm = _tile(rows_per_batch, 256)
    tiles_per_batch = rows_per_batch // tm
    modv3 = modv.reshape(modv.shape[0], 1, modv.shape[1])
    kern = functools.partial(_ln_route_kernel, n_exp=n_exp)
    vmem = 6 * tm * d * 4 + 6 * tm * d * 4 + (8 << 20)
    small = lambda dt: jax.ShapeDtypeStruct((TOP_K, r), dt)
    return pl.pallas_call(
        kern,
        grid=(r // tm,),
        in_specs=[pl.BlockSpec((tm, d), lambda i: (i, 0)),
                  pl.BlockSpec((1, d), lambda i: (0, 0)),
                  pl.BlockSpec((1, d), lambda i: (0, 0)),
                  pl.BlockSpec((1, 1, d), lambda i: (i // tiles_per_batch, 0, sc_col)),
                  pl.BlockSpec((1, 1, d), lambda i: (i // tiles_per_batch, 0, sh_col)),
                  pl.BlockSpec((n_exp, d), lambda i: (0, 0)),
                  pl.BlockSpec((n_exp, 1), lambda i: (0, 0))],
        out_specs=[pl.BlockSpec((tm, d), lambda i: (i, 0)),
                   pl.BlockSpec((tm, d), lambda i: (i, 0)),
                   pl.BlockSpec((TOP_K, tm), lambda i: (0, i)),
                   pl.BlockSpec((TOP_K, tm), lambda i: (0, i)),
                   pl.BlockSpec((TOP_K, tm), lambda i: (0, i)),
                   pl.BlockSpec((n_exp, LANES), lambda i: (0, 0))],
        out_shape=[jax.ShapeDtypeStruct((r, d), F32),
                   jax.ShapeDtypeStruct((r, d), F32),
                   small(jnp.int32), small(F32), small(jnp.int32),
                   jax.ShapeDtypeStruct((n_exp, LANES), jnp.int32)],
        scratch_shapes=[pltpu.VMEM((n_exp, LANES), F32)],
        compiler_params=_params(("arbitrary",), vmem),
        name="ln_route",
    )(r2, ln_g.reshape(1, d), ln_b.reshape(1, d), modv3, modv3, wr_t, b_r.reshape(n_exp, 1))


def _deinterleave_kernel(w_ref, o_ref, *, f, grp):
    half = grp // 2
    s_idx = lax.broadcasted_iota(jnp.int32, (grp, grp), 0)
    t_idx = lax.broadcasted_iota(jnp.int32, (grp, grp), 1)
    src = jnp.where(t_idx < half, 2 * t_idx, 2 * (t_idx - half) + 1)
    perm = jnp.where(s_idx == src, 1.0, 0.0).astype(BF16)
    for g in range(2 * f // grp):
        p = _dot(w_ref[0, :, g * grp:(g + 1) * grp].astype(BF16), perm)
        o_ref[0, :, g * half:(g + 1) * half] = p[:, :half].astype(BF16)
        o_ref[0, :, f + g * half:f + (g + 1) * half] = p[:, half:].astype(BF16)


def _deinterleave_gate_up(w_gu):
    n_exp, d, f2 = w_gu.shape
    tr = _tile(d, 1024)
    grp = min(f2, 2 * LANES)
    kern = functools.partial(_deinterleave_kernel, f=f2 // 2, grp=grp)
    vmem = 2 * (tr * f2 * 4 + tr * f2 * 2) + 4 * tr * grp * 4 + (4 << 20)
    return pl.pallas_call(
        kern,
        grid=(n_exp, d // tr),
        in_specs=[pl.BlockSpec((1, tr, f2), lambda e, i: (e, i, 0))],
        out_specs=pl.BlockSpec((1, tr, f2), lambda e, i: (e, i, 0)),
        out_shape=jax.ShapeDtypeStruct((n_exp, d, f2), BF16),
        compiler_params=_params(("parallel", "parallel"), vmem),
        name="wgu_prep",
    )(w_gu)


def _start_row_gather(idx_ref, n_rows, src_ref, dst_ref, sem, unrolled):
    def start(r):
        pltpu.make_async_copy(src_ref.at[pl.ds(idx_ref[0, 0, r], 1)], dst_ref.at[pl.ds(r, 1)], sem).start()

    if unrolled:
        for r in range(n_rows):
            start(r)
    else:
        def body(r, carry):
            start(r)
            return carry
        lax.fori_loop(0, n_rows, body, 0, unroll=8)


def _wait_row_gather(dst_ref, sem):
    pltpu.make_async_copy(dst_ref, dst_ref, sem).wait()


GATHER_AHEAD = 2
GATHER_SLOTS = GATHER_AHEAD + 1


def _gather_index_specs(tm, n_tiles, index_of):
    assert n_tiles > GATHER_AHEAD
    return [pl.BlockSpec((1, 1, tm), index_of(a), memory_space=pltpu.SMEM) for a in range(GATHER_SLOTS)]


def _ring_gather_step(i, n_tiles, idx_refs, n_rows, src_ref, buf_ref, sems, ahead_unrolled):
    @pl.when(i == 0)
    def _():
        for a in range(GATHER_AHEAD):
            _start_row_gather(idx_refs[a], n_rows, src_ref, buf_ref.at[a], sems.at[a], False)

    slot = lax.rem(i, GATHER_SLOTS)
    ahead = lax.rem(i + GATHER_AHEAD, GATHER_SLOTS)
    _wait_row_gather(buf_ref.at[slot], sems.at[slot])

    def start_ahead(unrolled=ahead_unrolled):
        _start_row_gather(idx_refs[GATHER_AHEAD], n_rows, src_ref, buf_ref.at[ahead], sems.at[ahead], unrolled)

    return slot, start_ahead


def _drain_ring(i, n_tiles, buf_ref, sems):
    @pl.when(i == n_tiles - 1)
    def _():
        for a in range(1, GATHER_SLOTS):
            s = (n_tiles - 1 + a) % GATHER_SLOTS
            _wait_row_gather(buf_ref.at[s], sems.at[s])


def _moe_up_kernel(be_ref, nu_ref, idx0_ref, idx1_ref, idx2_ref, tok_ref, w_ref, bg_ref, bl_ref,
                   h_ref, xbuf, sems, *, tm, n_tiles, f):
    i = pl.program_id(0)
    slot, start_ahead = _ring_gather_step(i, n_tiles, (idx0_ref, idx1_ref, idx2_ref), tm, tok_ref,
                                          xbuf, sems, True)

    @pl.when(i < nu_ref[0])
    def _():
        start_ahead()
        x = xbuf[slot].astype(BF16)
        glu = jnp.minimum(_dot(x, w_ref[0, :, :f]) + bg_ref[0], SWIGLU_LIMIT)
        lin = jnp.clip(_dot(x, w_ref[0, :, f:]) + bl_ref[0], -SWIGLU_LIMIT, SWIGLU_LIMIT)
        h_ref[...] = (glu * jax.nn.sigmoid(SWIGLU_ALPHA * glu) * (lin + 1.0)).astype(BF16)

    @pl.when(i >= nu_ref[0])
    def _():
        start_ahead(False)
        h_ref[...] = jnp.zeros_like(h_ref)

    _drain_ring(i, n_tiles, xbuf, sems)


def _moe_up(tokens, row_idx, blk_exp, n_used, wgu_bf, bg, bl, tm):
    n = row_idx.shape[0]
    d = tokens.shape[1]
    n_exp, _, f2 = wgu_bf.shape
    f = f2 // 2
    n_tiles = n // tm
    idx3 = row_idx.reshape(n_tiles, 1, tm)
    kern = functools.partial(_moe_up_kernel, tm=tm, n_tiles=n_tiles, f=f)
    vmem = GATHER_SLOTS * tm * d * 4 + 2 * d * f2 * 2 + 2 * tm * f * 2 + 6 * tm * f * 4 + tm * d * 2 \
        + (4 << 20)
    idx_specs = _gather_index_specs(
        tm, n_tiles, lambda a: (lambda i, be, nu: (jnp.minimum(i + a, n_tiles - 1), 0, 0)))
    grid_spec = pltpu.PrefetchScalarGridSpec(
        num_scalar_prefetch=2,
        grid=(n_tiles,),
        in_specs=idx_specs + [
            pl.BlockSpec(memory_space=pl.ANY),
            pl.BlockSpec((1, d, f2), lambda i, be, nu: (be[i], 0, 0)),
            pl.BlockSpec((1, 1, f), lambda i, be, nu: (be[i], 0, 0)),
            pl.BlockSpec((1, 1, f), lambda i, be, nu: (be[i], 0, 0))],
        out_specs=pl.BlockSpec((tm, f), lambda i, be, nu: (i, 0)),
        scratch_shapes=[pltpu.VMEM((GATHER_SLOTS, tm, d), tokens.dtype),
                        pltpu.SemaphoreType.DMA((GATHER_SLOTS,))])
    return pl.pallas_call(
        kern,
        grid_spec=grid_spec,
        out_shape=jax.ShapeDtypeStruct((n, f), BF16),
        compiler_params=_params(("arbitrary",), vmem),
        name="moe_up",
    )(blk_exp, n_used, idx3, idx3, idx3, tokens, wgu_bf, bg.reshape(n_exp, 1, f), bl.reshape(n_exp, 1, f))


def _moe_down_kernel(be_ref, nu_ref, h_ref, w_ref, b_ref, y_ref):
    i = pl.program_id(0)

    @pl.when(i < nu_ref[0])
    def _():
        y_ref[...] = _dot(h_ref[...], w_ref[0]) + b_ref[0]

    @pl.when(i >= nu_ref[0])
    def _():
        y_ref[...] = jnp.zeros_like(y_ref)


def _moe_down(hs, blk_exp, n_used, wd_bf, bd, tm):
    n, f = hs.shape
    n_exp, _, d = wd_bf.shape
    vmem = 2 * (tm * f * 2 + f * d * 2 + tm * d * 4) + 2 * tm * d * 4 + (4 << 20)
    grid_spec = pltpu.PrefetchScalarGridSpec(
        num_scalar_prefetch=2,
        grid=(n // tm,),
        in_specs=[pl.BlockSpec((tm, f), lambda i, be, nu: (i, 0)),
                  pl.BlockSpec((1, f, d), lambda i, be, nu: (be[i], 0, 0)),
                  pl.BlockSpec((1, 1, d), lambda i, be, nu: (be[i], 0, 0))],
        out_specs=pl.BlockSpec((tm, d), lambda i, be, nu: (i, 0)))
    return pl.pallas_call(
        _moe_down_kernel,
        grid_spec=grid_spec,
        out_shape=jax.ShapeDtypeStruct((n, d), F32),
        compiler_params=_params(("arbitrary",), vmem),
        name="moe_down",
    )(blk_exp, n_used, hs, wd_bf, bd.reshape(n_exp, 1, d))


def _combine_kernel(pos0_ref, pos1_ref, pos2_ref, y_ref, gate_ref, x1_ref, g2_ref, lg_ref, lb_ref,
                    o_ref, buf_ref, sems, *, tm, n_steps):
    i = pl.program_id(0)
    slot, start_ahead = _ring_gather_step(i, n_steps, (pos0_ref, pos1_ref, pos2_ref), TOP_K * tm, y_ref,
                                          buf_ref, sems, True)
    start_ahead()
    gates = gate_ref[...]
    moe = gates[:, 0:1] * buf_ref[slot, 0:tm, :]
    for k in range(1, TOP_K):
        moe = moe + gates[:, k:k + 1] * buf_ref[slot, k * tm:(k + 1) * tm, :]
    pre = DN_ALPHA * x1_ref[...] + g2_ref[0] * moe
    o_ref[...] = _layer_norm_rows(pre, lg_ref[...], lb_ref[...])
    _drain_ring(i, n_steps, buf_ref, sems)


def _combine(y, pos, gates_t, x1, modv, ln_g, ln_b, rows_per_batch, g_col):
    r, d = x1.shape
    tm = _tile(rows_per_batch, 128)
    tiles_per_batch = rows_per_batch // tm
    n_steps = r // tm
    modv3 = modv.reshape(modv.shape[0], 1, modv.shape[1])
    pos_steps = pos.reshape(TOP_K, n_steps, tm).transpose(1, 0, 2).reshape(n_steps, 1, TOP_K * tm)
    kern = functools.partial(_combine_kernel, tm=tm, n_steps=n_steps)
    vmem = GATHER_SLOTS * TOP_K * tm * d * 4 + 4 * tm * d * 4 + 4 * tm * d * 4 + (4 << 20)
    pos_specs = _gather_index_specs(
        TOP_K * tm, n_steps, lambda a: (lambda i: (jnp.minimum(i + a, n_steps - 1), 0, 0)))
    return pl.pallas_call(
        kern,
        grid=(n_steps,),
        in_specs=pos_specs + [
            pl.BlockSpec(memory_space=pl.ANY),
            pl.BlockSpec((tm, TOP_K), lambda i: (i, 0)),
            pl.BlockSpec((tm, d), lambda i: (i, 0)),
            pl.BlockSpec((1, 1, d), lambda i: (i // tiles_per_batch, 0, g_col)),
            pl.BlockSpec((1, d), lambda i: (0, 0)),
            pl.BlockSpec((1, d), lambda i: (0, 0))],
        out_specs=pl.BlockSpec((tm, d), lambda i: (i, 0)),
        out_shape=jax.ShapeDtypeStruct((r, d), F32),
        scratch_shapes=[pltpu.VMEM((GATHER_SLOTS, TOP_K * tm, d), F32),
                        pltpu.SemaphoreType.DMA((GATHER_SLOTS,))],
        compiler_params=_params(("arbitrary",), vmem),
        name="combine",
    )(pos_steps, pos_steps, pos_steps, y, gates_t, x1, modv3, ln_g.reshape(1, d), ln_b.reshape(1, d))


def kernel(x, c, ctx, c_ctx, w_mod, b_mod, w_in, conv_w, conv_b, lru_wa, lru_ba, lru_wx, lru_bx,
           lru_lambda, sg_ln_g, sg_ln_b, sg_w, sg_b, w_branch_a, w_branch_b, w_out, ln1_g, ln1_b,
           w_router, b_router, w_gate_up, b_gate_up, w_down, b_down, ln2_g, ln2_b):
    assert w_mod.shape[0] == DEPTH
    batch, seq, d = x.shape
    ctx_len = ctx.shape[1]
    w_lru = conv_w.shape[-1]
    n_heads = lru_wa.shape[2]
    hd = w_lru // n_heads
    w_sg = sg_ln_g.shape[-1]
    n_exp = w_router.shape[-1]
    n_tok = batch * seq
    l = 0

    pad = (-(batch + 1)) % SUBLANES
    c_rows = jnp.concatenate([c, c_ctx[None, :], jnp.zeros((pad, d), F32)], axis=0)
    modv = _mod(c_rows, w_mod[l], b_mod[l])

    w_in_bf = w_in[l].astype(BF16)
    n_in = w_in_bf.shape[1]
    col_gy, col_u, col_v = w_lru, 2 * w_lru, 2 * w_lru + w_sg
    col_ga = 2 * w_lru + 2 * w_sg
    col_gb = col_ga + d
    x2 = x.reshape(n_tok, d)
    z = _inproj(x2, modv, w_in_bf, seq, 0, n_in, col_gy, col_ga)
    zc = _inproj(ctx.reshape(batch * ctx_len, d), modv, w_in_bf, batch * ctx_len, batch,
                 w_lru, w_lru, w_lru)

    wax = jnp.concatenate([lru_wa[l], lru_wx[l]], axis=-1).astype(BF16)
    y_a = _lru(z, zc, conv_w[l], conv_b[l], wax, lru_ba[l].reshape(2, 1, w_lru),
               lru_bx[l].reshape(2, 1, w_lru), lru_lambda[l].reshape(2, 1, w_lru),
               batch, seq, ctx_len, w_lru, hd, col_gy)
    y_b = _spatial_gating(z, sg_ln_g[l], sg_ln_b[l], sg_w[l].astype(BF16), sg_b[l].T,
                          n_tok, w_sg, col_u, col_v)

    merged = _merge(y_a, y_b, w_branch_a[l].astype(BF16), w_branch_b[l].astype(BF16), z, col_ga, col_gb)
    r1 = _outproj(merged, w_out[l].astype(BF16), x2, modv, seq, 2)
    x1, tokens, idx_t, gate_t, rank_t, counts = _ln_route(
        r1, ln1_g[l], ln1_b[l], modv, w_router[l].T, b_router[l], seq, 4, 3)

    tm_moe = 256
    n_assign = n_tok * TOP_K
    n_tiles = -(-(n_assign + n_exp * (tm_moe - 1)) // tm_moe)
    n_pad = n_tiles * tm_moe
    cnt = counts[:, 0]
    padded = (cnt + tm_moe - 1) // tm_moe * tm_moe
    pad_end = jnp.cumsum(padded)
    pad_start = pad_end - padded
    expert_ids = jnp.arange(n_exp, dtype=jnp.int32)
    pos = rank_t + jnp.sum(jnp.where(idx_t[..., None] == expert_ids, pad_start, 0), axis=-1)
    tok_ids = jnp.broadcast_to(jnp.arange(n_tok, dtype=jnp.int32)[None, :], (TOP_K, n_tok))
    tok_pad = jnp.zeros((n_pad,), jnp.int32).at[pos.reshape(-1)].set(tok_ids.reshape(-1))
    blk_start = jnp.arange(n_tiles, dtype=jnp.int32) * tm_moe
    blk_exp = jnp.minimum(jnp.sum(blk_start[:, None] >= pad_end[None, :], axis=1), n_exp - 1).astype(jnp.int32)
    n_used = (pad_end[-1] // tm_moe).astype(jnp.int32).reshape(1)

    wgu = _deinterleave_gate_up(w_gate_up[l])
    hs = _moe_up(tokens, tok_pad, blk_exp, n_used, wgu,
                 b_gate_up[l][..., ::2], b_gate_up[l][..., 1::2], tm_moe)
    ys = _moe_down(hs, blk_exp, n_used, w_down[l].astype(BF16), b_down[l], tm_moe)

    out = _combine(ys, pos, gate_t.T, x1, modv, ln2_g[l], ln2_b[l], seq, 5)
    return out.reshape(batch, seq, d)
```

```python
import functools

import jax
import jax.numpy as jnp
from jax import lax
from jax.experimental import pallas as pl
from jax.experimental.pallas import tpu as pltpu

TOP_K = 4
LRU_C = 8.0
CONV_WIDTH = 4
CONV_PAD_LEFT = 1
SWIGLU_LIMIT = 7.0
SWIGLU_ALPHA = 1.702
DEPTH = 1
DN_ALPHA = (2.0 * DEPTH) ** 0.25
LN_EPS = 1e-5
N_MOD = 6

SUBLANES = 8
LANES = 128
V7X_VMEM_CAP = 60000 * 1024

F32 = jnp.float32
BF16 = jnp.bfloat16


def _params(semantics, vmem_bytes):
    return pltpu.CompilerParams(dimension_semantics=semantics,
                                vmem_limit_bytes=int(min(vmem_bytes, V7X_VMEM_CAP)))


def _tile(n, pref):
    t = min(n, pref)
    while n % t:
        t //= 2
    return t


def _dot(a, b):
    return jnp.dot(a, b, preferred_element_type=F32)


def _layer_norm_rows(x, g, b):
    mu = jnp.mean(x, axis=-1, keepdims=True)
    xc = x - mu
    var = jnp.mean(xc * xc, axis=-1, keepdims=True)
    return xc * lax.rsqrt(var + LN_EPS) * g + b


def _mod_kernel(c_ref, w_ref, b_ref, o_ref):
    c = c_ref[...]
    s = c * jax.nn.sigmoid(c)
    o_ref[...] = _dot(s.astype(BF16), w_ref[...].astype(BF16)) + b_ref[...]


def _mod(c_rows, w_mod, b_mod):
    m, d = c_rows.shape
    n = w_mod.shape[1]
    tn = _tile(n, 512)
    return pl.pallas_call(
        _mod_kernel,
        grid=(n // tn,),
        in_specs=[pl.BlockSpec((m, d), lambda j: (0, 0)),
                  pl.BlockSpec((d, tn), lambda j: (0, j)),
                  pl.BlockSpec((1, tn), lambda j: (0, j))],
        out_specs=pl.BlockSpec((m, tn), lambda j: (0, j)),
        out_shape=jax.ShapeDtypeStruct((m, n), F32),
        compiler_params=_params(("arbitrary",), 3 * d * tn * 4 + (4 << 20)),
        name="mod",
    )(c_rows, w_mod, b_mod.reshape(1, n))


def _inproj_kernel(x_ref, sc_ref, sh_ref, w_ref, o_ref, h_ref, *, n_plain, n_gelu):
    j = pl.program_id(1)

    @pl.when(j == 0)
    def _():
        h = x_ref[...] * (1.0 + sc_ref[0]) + sh_ref[0]
        h_ref[...] = h.astype(BF16)

    @pl.when(j < n_plain)
    def _():
        o_ref[...] = _dot(h_ref[...], w_ref[...])

    @pl.when(jnp.logical_and(j >= n_plain, j < n_gelu))
    def _():
        o_ref[...] = jax.nn.gelu(_dot(h_ref[...], w_ref[...]))

    @pl.when(j >= n_gelu)
    def _():
        o_ref[...] = jax.nn.sigmoid(_dot(h_ref[...], w_ref[...]))


def _inproj(x2, modv, w_bf, rows_per_batch, mod_row0, n_cols, plain_cols, gelu_cols):
    r, d = x2.shape
    tm = _tile(rows_per_batch, 512)
    tn = _tile(_gcd_all(n_cols, plain_cols, gelu_cols), 1024)
    tiles_per_batch = rows_per_batch // tm
    modv3 = modv.reshape(modv.shape[0], 1, modv.shape[1])

    def mod_map(col):
        return lambda i, j: (mod_row0 + i // tiles_per_batch, 0, col)

    kern = functools.partial(_inproj_kernel, n_plain=plain_cols // tn, n_gelu=gelu_cols // tn)
    vmem = 2 * tm * d * 4 + tm * d * 2 + 2 * d * tn * 2 + 4 * tm * tn * 4 + (4 << 20)
    return pl.pallas_call(
        kern,
        grid=(r // tm, n_cols // tn),
        in_specs=[pl.BlockSpec((tm, d), lambda i, j: (i, 0)),
                  pl.BlockSpec((1, 1, d), mod_map(1)),
                  pl.BlockSpec((1, 1, d), mod_map(0)),
                  pl.BlockSpec((d, tn), lambda i, j: (0, j))],
        out_specs=pl.BlockSpec((tm, tn), lambda i, j: (i, j)),
        out_shape=jax.ShapeDtypeStruct((r, n_cols), F32),
        scratch_shapes=[pltpu.VMEM((tm, d), BF16)],
        compiler_params=_params(("parallel", "arbitrary"), vmem),
        name="inproj",
    )(x2, modv3, modv3, w_bf)


def _gcd_all(*vals):
    import math
    g = 0
    for v in vals:
        g = math.gcd(g, v)
    return g


def _deinterleave_perm(grp):
    half = grp // 2
    s_idx = lax.broadcasted_iota(jnp.int32, (grp, grp), 0)
    t_idx = lax.broadcasted_iota(jnp.int32, (grp, grp), 1)
    src = jnp.where(t_idx < half, 2 * t_idx, 2 * (t_idx - half) + 1)
    return jnp.where(s_idx == src, 1.0, 0.0).astype(BF16)


def _deinterleave_rows(w_rows, perm, f, store):
    grp = perm.shape[0]
    half = grp // 2
    for g in range(2 * f // grp):
        p = _dot(w_rows[:, g * grp:(g + 1) * grp].astype(BF16), perm)
        store(g * half, p[:, :half].astype(BF16))
        store(f + g * half, p[:, half:].astype(BF16))


class _WeightPrepStream:
    def __init__(self, src_ref, dst_ref, win_ref, wout_ref, sems_in, sems_out, row0, chunk_rows, n_chunks):
        self.src, self.dst, self.win, self.wout = src_ref, dst_ref, win_ref, wout_ref
        self.sems_in, self.sems_out = sems_in, sems_out
        self.row0, self.cr, self.n = row0, chunk_rows, n_chunks
        self.f = src_ref.shape[1] // 2
        self.perm = _deinterleave_perm(min(src_ref.shape[1], 2 * LANES))

    def _rows(self, c):
        return pl.ds(pl.multiple_of(self.row0 + c * self.cr, self.cr), self.cr)

    def _in_copy(self, c, slot):
        return pltpu.make_async_copy(self.src.at[self._rows(c)], self.win.at[slot], self.sems_in.at[slot])

    def _out_copy(self, c, slot):
        return pltpu.make_async_copy(self.wout.at[slot], self.dst.at[self._rows(c)], self.sems_out.at[slot])

    def prologue(self):
        for c in range(min(2, self.n)):
            self._in_copy(c, c).start()

    def begin(self, c):
        slot = lax.rem(c, 2)

        @pl.when(c >= 2)
        def _():
            self._out_copy(c - 2, slot).wait()

        self._in_copy(c, slot).wait()

    def convert(self, c):
        slot = lax.rem(c, 2)

        def store(col0, block):
            self.wout[slot, :, col0:col0 + block.shape[1]] = block

        _deinterleave_rows(self.win[slot], self.perm, self.f, store)

    def end(self, c):
        slot = lax.rem(c, 2)
        self._out_copy(c, slot).start()

        @pl.when(c + 2 < self.n)
        def _():
            self._in_copy(c + 2, slot).start()

    def epilogue(self):
        for c in range(max(self.n - 2, 0), self.n):
            self._out_copy(c, c % 2).wait()


def _lru_kernel(zr_ref, gy_ref, zc_ref, cw_ref, cb_ref, wax_ref, ba_ref, bx_ref, lam_ref, wgu_ref,
                o_ref, wgu_out_ref, hf_ref, hb_ref, win_ref, wout_ref, sems_in, sems_out,
                *, seq, ctx_len, rows, cb, hd, prep_rows):
    n_heads = cb // hd
    groups = rows // SUBLANES
    cw = cw_ref[...]
    cbias = cb_ref[...]
    row_in_group = lax.broadcasted_iota(jnp.int32, (groups, SUBLANES, cb), 1)
    n_chunks = seq // rows
    step_id = pl.program_id(0) * pl.num_programs(1) + pl.program_id(1)
    prep = _WeightPrepStream(wgu_ref, wgu_out_ref, win_ref, wout_ref, sems_in, sems_out,
                             step_id * (n_chunks * prep_rows), prep_rows, n_chunks)

    def conv_chunk(ref, r0, total):
        p0 = jnp.maximum(r0 - SUBLANES, 0)
        prev = jnp.where(r0 > 0, ref[pl.ds(pl.multiple_of(p0, SUBLANES), SUBLANES), :], 0.0)
        n0 = jnp.minimum(r0 + rows, total - SUBLANES)
        nxt = jnp.where(r0 + rows < total, ref[pl.ds(pl.multiple_of(n0, SUBLANES), SUBLANES), :], 0.0)
        cur = ref[pl.ds(r0, rows), :]
        ext = jnp.concatenate([prev, cur, nxt], axis=0)
        n_ext = rows + 2 * SUBLANES
        u = cbias + cw[1:2] * cur
        u = u + cw[0:1] * pltpu.roll(ext, 1, axis=0)[SUBLANES:SUBLANES + rows]
        u = u + cw[2:3] * pltpu.roll(ext, n_ext - 1, axis=0)[SUBLANES:SUBLANES + rows]
        u = u + cw[3:4] * pltpu.roll(ext, n_ext - 2, axis=0)[SUBLANES:SUBLANES + rows]
        return u

    def gates(u, direction):
        ub = u.astype(BF16)
        pre_a, pre_x = [], []
        for h in range(n_heads):
            g = _dot(ub[:, h * hd:(h + 1) * hd], wax_ref[direction, h])
            pre_a.append(g[:, :hd])
            pre_x.append(g[:, hd:])
        pre_a = pre_a[0] if n_heads == 1 else jnp.concatenate(pre_a, axis=1)
        pre_x = pre_x[0] if n_heads == 1 else jnp.concatenate(pre_x, axis=1)
        r = jax.nn.sigmoid(pre_a + ba_ref[direction])
        i = jax.nn.sigmoid(pre_x + bx_ref[direction])
        neg_lam = -lam_ref[direction]
        softplus = jnp.maximum(neg_lam, 0.0) + jnp.log1p(jnp.exp(-jnp.abs(neg_lam)))
        log_a = -LRU_C * r * softplus
        a = jnp.exp(log_a)
        q = jnp.tanh(-log_a) * (1.0 + a * a)
        mult = jnp.where(q > 0.0, q * lax.rsqrt(q), 0.0)
        return a, mult * i * u

    def local_scan(a, b, reverse):
        a = a.reshape(groups, SUBLANES, cb)
        b = b.reshape(groups, SUBLANES, cb)
        d = 1
        while d < SUBLANES:
            if reverse:
                shift, ok = SUBLANES - d, row_in_group < SUBLANES - d
            else:
                shift, ok = d, row_in_group >= d
            a_s = pltpu.roll(a, shift, axis=1)
            b_s = pltpu.roll(b, shift, axis=1)
            b = b + jnp.where(ok, a * b_s, 0.0)
            a = jnp.where(ok, a * a_s, a)
            d *= 2
        return a.reshape(rows, cb), b.reshape(rows, cb)

    def scan_chunk(ref, r0, total, direction, reverse, h, dst_ref):
        u = conv_chunk(ref, r0, total)
        a, b = gates(u, direction)
        a, b = local_scan(a, b, reverse)
        order = range(groups - 1, -1, -1) if reverse else range(groups)
        for g in order:
            sl = slice(g * SUBLANES, (g + 1) * SUBLANES)
            hg = b[sl] + a[sl] * h
            if dst_ref is not None:
                dst_ref[pl.ds(pl.multiple_of(r0 + g * SUBLANES, SUBLANES), SUBLANES), :] = hg
            h = hg[0:1] if reverse else hg[SUBLANES - 1:SUBLANES]
        return h

    def sweep(ref, total, h_fwd, h_bwd, fwd_dst, bwd_dst, side_work=None):
        n = total // rows

        def body(c, carry):
            hf, hb = carry
            if side_work is not None:
                side_work.begin(c)
                side_work.convert(c)
            hf = scan_chunk(ref, pl.multiple_of(c * rows, rows), total, 0, False, hf, fwd_dst)
            hb = scan_chunk(ref, pl.multiple_of((n - 1 - c) * rows, rows), total, 1, True, hb, bwd_dst)
            if side_work is not None:
                side_work.end(c)
            return hf, hb

        return lax.fori_loop(0, n, body, (h_fwd, h_bwd))

    zero = jnp.zeros((1, cb), F32)
    prep.prologue()
    h_f, h_b = sweep(zc_ref, ctx_len, zero, zero, None, None)
    sweep(zr_ref, seq, h_f, h_b, hf_ref, hb_ref, prep)
    prep.epilogue()

    def out_body(c, carry):
        sl = pl.ds(pl.multiple_of(c * rows, rows), rows)
        o_ref[sl, :] = ((hf_ref[sl, :] + hb_ref[sl, :]) * gy_ref[sl, :]).astype(BF16)
        return carry

    lax.fori_loop(0, seq // rows, out_body, 0)


def _lru(z, zc, conv_w, conv_b, wax, ba, bx, lam, w_gate_up, batch, seq, ctx_len, w_lru, hd, gy_col0):
    cb = _tile(w_lru, 2 * hd)
    rows = _tile(_gcd_all(seq, ctx_len), 256)
    n_cb = w_lru // cb
    n_exp, d, f2 = w_gate_up.shape
    wgu_rows = w_gate_up.reshape(n_exp * d, f2)
    n_prep_chunks = batch * n_cb * (seq // rows)
    assert (n_exp * d) % n_prep_chunks == 0
    prep_rows = (n_exp * d) // n_prep_chunks
    assert prep_rows % (2 * SUBLANES) == 0
    kern = functools.partial(_lru_kernel, seq=seq, ctx_len=ctx_len, rows=rows, cb=cb, hd=hd,
                             prep_rows=prep_rows)
    vmem = 4 * seq * cb * 4 + 2 * seq * cb * 2 + 2 * seq * cb * 4 + 80 * rows * cb * 4 \
        + 2 * prep_rows * f2 * (4 + 2) + 4 * prep_rows * f2 * 4 + (4 << 20)
    y_a, wgu_bf = pl.pallas_call(
        kern,
        grid=(batch, n_cb),
        in_specs=[pl.BlockSpec((seq, cb), lambda b, j: (b, j)),
                  pl.BlockSpec((seq, cb), lambda b, j: (b, gy_col0 // cb + j)),
                  pl.BlockSpec((ctx_len, cb), lambda b, j: (b, j)),
                  pl.BlockSpec((CONV_WIDTH, cb), lambda b, j: (0, j)),
                  pl.BlockSpec((1, cb), lambda b, j: (0, j)),
                  pl.BlockSpec((2, cb // hd, hd, 2 * hd), lambda b, j: (0, j, 0, 0)),
                  pl.BlockSpec((2, 1, cb), lambda b, j: (0, 0, j)),
                  pl.BlockSpec((2, 1, cb), lambda b, j: (0, 0, j)),
                  pl.BlockSpec((2, 1, cb), lambda b, j: (0, 0, j)),
                  pl.BlockSpec(memory_space=pl.ANY)],
        out_specs=[pl.BlockSpec((seq, cb), lambda b, j: (b, j)),
                   pl.BlockSpec(memory_space=pl.ANY)],
        out_shape=[jax.ShapeDtypeStruct((batch * seq, w_lru), BF16),
                   jax.ShapeDtypeStruct((n_exp * d, f2), BF16)],
        scratch_shapes=[pltpu.VMEM((seq, cb), F32), pltpu.VMEM((seq, cb), F32),
                        pltpu.VMEM((2, prep_rows, f2), F32), pltpu.VMEM((2, prep_rows, f2), BF16),
                        pltpu.SemaphoreType.DMA((2,)), pltpu.SemaphoreType.DMA((2,))],
        compiler_params=_params(("arbitrary", "arbitrary"), vmem),
        name="lru",
    )(z, z, zc, conv_w, conv_b.reshape(1, w_lru), wax, ba, bx, lam, wgu_rows)
    return y_a, wgu_bf.reshape(n_exp, d, f2)


def _sg_kernel(u_ref, v_ref, g_ref, b_ref, sw_ref, sbt_ref, o_ref, *, n_sub, chunk, n_groups, gd):
    vn = _layer_norm_rows(v_ref[...], g_ref[...], b_ref[...]).astype(BF16)
    for c in range(n_sub):
        rs = slice(c * chunk, (c + 1) * chunk)
        for g in range(n_groups):
            cs = slice(g * gd, (g + 1) * gd)
            mixed = _dot(sw_ref[g], vn[rs, cs]) + sbt_ref[:, g:g + 1]
            o_ref[rs, cs] = (u_ref[rs, cs] * mixed).astype(BF16)


def _spatial_gating(z, ln_g, ln_b, sw_bf, sb_t, n_rows, w_sg, u_col0, v_col0):
    n_groups, chunk, _ = sw_bf.shape
    gd = w_sg // n_groups
    n_sub = _tile(n_rows // chunk, 4)
    tm = n_sub * chunk
    kern = functools.partial(_sg_kernel, n_sub=n_sub, chunk=chunk, n_groups=n_groups, gd=gd)
    vmem = 4 * tm * w_sg * 4 + 2 * tm * w_sg * 2 + 6 * tm * w_sg * 4 + (4 << 20)
    return pl.pallas_call(
        kern,
        grid=(n_rows // tm,),
        in_specs=[pl.BlockSpec((tm, w_sg), lambda i: (i, u_col0 // w_sg)),
                  pl.BlockSpec((tm, w_sg), lambda i: (i, v_col0 // w_sg)),
                  pl.BlockSpec((1, w_sg), lambda i: (0, 0)),
                  pl.BlockSpec((1, w_sg), lambda i: (0, 0)),
                  pl.BlockSpec((n_groups, chunk, chunk), lambda i: (0, 0, 0)),
                  pl.BlockSpec((chunk, n_groups), lambda i: (0, 0))],
        out_specs=pl.BlockSpec((tm, w_sg), lambda i: (i, 0)),
        out_shape=jax.ShapeDtypeStruct((n_rows, w_sg), BF16),
        compiler_params=_params(("parallel",), vmem),
        name="sg",
    )(z, z, ln_g.reshape(1, w_sg), ln_b.reshape(1, w_sg), sw_bf, sb_t)


def _merge_kernel(ya_ref, yb_ref, wa_ref, wb_ref, ga_ref, gb_ref, o_ref):
    pa = _dot(ya_ref[...], wa_ref[...])
    pb = _dot(yb_ref[...], wb_ref[...])
    o_ref[...] = (ga_ref[...] * pa + gb_ref[...] * pb).astype(BF16)


def _merge(ya, yb, wa_bf, wb_bf, z, ga_col0, gb_col0):
    r, ka = ya.shape
    kb = yb.shape[1]
    d = wa_bf.shape[1]
    tm = _tile(r, 1024)
    tn = _tile(d, 512)
    vmem = 2 * (tm * (ka + kb) * 2 + (ka + kb) * tn * 2 + 2 * tm * tn * 4 + tm * tn * 2) \
        + 4 * tm * tn * 4 + (4 << 20)
    return pl.pallas_call(
        _merge_kernel,
        grid=(r // tm, d // tn),
        in_specs=[pl.BlockSpec((tm, ka), lambda i, j: (i, 0)),
                  pl.BlockSpec((tm, kb), lambda i, j: (i, 0)),
                  pl.BlockSpec((ka, tn), lambda i, j: (0, j)),
                  pl.BlockSpec((kb, tn), lambda i, j: (0, j)),
                  pl.BlockSpec((tm, tn), lambda i, j: (i, ga_col0 // tn + j)),
                  pl.BlockSpec((tm, tn), lambda i, j: (i, gb_col0 // tn + j))],
        out_specs=pl.BlockSpec((tm, tn), lambda i, j: (i, j)),
        out_shape=jax.ShapeDtypeStruct((r, d), BF16),
        compiler_params=_params(("parallel", "parallel"), vmem),
        name="merge",
    )(ya, yb, wa_bf, wb_bf, z, z)


def _outproj_kernel(m_ref, w_ref, x_ref, g_ref, o_ref):
    o_ref[...] = DN_ALPHA * x_ref[...] + g_ref[0] * _dot(m_ref[...], w_ref[...])


def _outproj(merged, wo_bf, x2, modv, rows_per_batch, g_col):
    r, k = merged.shape
    d = wo_bf.shape[1]
    tm = _tile(rows_per_batch, 1024)
    tn = _tile(d, 1024)
    tiles_per_batch = rows_per_batch // tm
    modv3 = modv.reshape(modv.shape[0], 1, modv.shape[1])
    vmem = 2 * (tm * k * 2 + k * tn * 2 + 2 * tm * tn * 4) + 2 * tm * tn * 4 + (4 << 20)
    return pl.pallas_call(
        _outproj_kernel,
        grid=(r // tm, d // tn),
        in_specs=[pl.BlockSpec((tm, k), lambda i, j: (i, 0)),
                  pl.BlockSpec((k, tn), lambda i, j: (0, j)),
                  pl.BlockSpec((tm, tn), lambda i, j: (i, j)),
                  pl.BlockSpec((1, 1, tn), lambda i, j: (i // tiles_per_batch, 0, g_col * (d // tn) + j))],
        out_specs=pl.BlockSpec((tm, tn), lambda i, j: (i, j)),
        out_shape=jax.ShapeDtypeStruct((r, d), F32),
        compiler_params=_params(("parallel", "parallel"), vmem),
        name="outproj",
    )(merged, wo_bf, x2, modv3)


def _ln_route_kernel(r_ref, g_ref, b_ref, sc_ref, sh_ref, wrt_ref, br_ref,
                     x1_ref, tok_ref, idx_ref, gate_ref, rank_ref, cnt_ref, run_ref, *, n_exp):
    i = pl.program_id(0)
    tm = r_ref.shape[0]

    @pl.when(i == 0)
    def _():
        run_ref[...] = jnp.zeros_like(run_ref)

    x1 = _layer_norm_rows(r_ref[...], g_ref[...], b_ref[...])
    x1_ref[...] = x1
    tok = x1 * (1.0 + sc_ref[0]) + sh_ref[0]
    tok_ref[...] = tok

    logits = lax.dot_general(wrt_ref[...], tok, (((1,), (1,)), ((), ())),
                             precision=lax.Precision.HIGHEST,
                             preferred_element_type=F32) + br_ref[...]
    eidx = lax.broadcasted_iota(jnp.int32, (n_exp, tm), 0)
    vals, idxs = [], []
    work = logits
    for _ in range(TOP_K):
        m = jnp.max(work, axis=0, keepdims=True)
        idx = jnp.min(jnp.where(work == m, eidx, n_exp), axis=0, keepdims=True)
        vals.append(m)
        idxs.append(idx)
        work = jnp.where(eidx == idx, -jnp.inf, work)
    exps = [jnp.exp(v - vals[0]) for v in vals]
    denom = exps[0]
    for e in exps[1:]:
        denom = denom + e
    gate_ref[...] = jnp.concatenate([e / denom for e in exps], axis=0)
    idx_ref[...] = jnp.concatenate(idxs, axis=0)

    sel = [eidx == idx for idx in idxs]
    onehot = sel[0]
    for s in sel[1:]:
        onehot = jnp.logical_or(onehot, s)
    onehot = jnp.where(onehot, 1.0, 0.0)
    s_idx = lax.broadcasted_iota(jnp.int32, (tm, tm), 0)
    t_idx = lax.broadcasted_iota(jnp.int32, (tm, tm), 1)
    strict_upper = jnp.where(s_idx < t_idx, 1.0, 0.0).astype(BF16)
    before = _dot(onehot.astype(BF16), strict_upper) + run_ref[:, 0:1]
    ranks = [jnp.sum(jnp.where(s, before, 0.0), axis=0, keepdims=True) for s in sel]
    rank_ref[...] = jnp.concatenate(ranks, axis=0).astype(jnp.int32)
    run_ref[...] = run_ref[...] + jnp.sum(onehot, axis=1, keepdims=True)
    cnt_ref[...] = run_ref[...].astype(jnp.int32)


def _ln_route(r2, ln_g, ln_b, modv, wr_t, b_r, rows_per_batch, sc_col, sh_col):
    r, d = r2.shape
    n_exp = wr_t.shape[0]
    tm = _tile(rows_per_batch, 256)
    tiles_per_batch = rows_per_batch // tm
    modv3 = modv.reshape(modv.shape[0], 1, modv.shape[1])
    kern = functools.partial(_ln_route_kernel, n_exp=n_exp)
    vmem = 6 * tm * d * 4 + 6 * tm * d * 4 + (8 << 20)
    small = lambda dt: jax.ShapeDtypeStruct((TOP_K, r), dt)
    return pl.pallas_call(
        kern,
        grid=(r // tm,),
        in_specs=[pl.BlockSpec((tm, d), lambda i: (i, 0)),
                  pl.BlockSpec((1, d), lambda i: (0, 0)),
                  pl.BlockSpec((1, d), lambda i: (0, 0)),
                  pl.BlockSpec((1, 1, d), lambda i: (i // tiles_per_batch, 0, sc_col)),
                  pl.BlockSpec((1, 1, d), lambda i: (i // tiles_per_batch, 0, sh_col)),
                  pl.BlockSpec((n_exp, d), lambda i: (0, 0)),
                  pl.BlockSpec((n_exp, 1), lambda i: (0, 0))],
        out_specs=[pl.BlockSpec((tm, d), lambda i: (i, 0)),
                   pl.BlockSpec((tm, d), lambda i: (i, 0)),
                   pl.BlockSpec((TOP_K, tm), lambda i: (0, i)),
                   pl.BlockSpec((TOP_K, tm), lambda i: (0, i)),
                   pl.BlockSpec((TOP_K, tm), lambda i: (0, i)),
                   pl.BlockSpec((n_exp, LANES), lambda i: (0, 0))],
        out_shape=[jax.ShapeDtypeStruct((r, d), F32),
                   jax.ShapeDtypeStruct((r, d), F32),
                   small(jnp.int32), small(F32), small(jnp.int32),
                   jax.ShapeDtypeStruct((n_exp, LANES), jnp.int32)],
        scratch_shapes=[pltpu.VMEM((n_exp, LANES), F32)],
        compiler_params=_params(("arbitrary",), vmem),
        name="ln_route",
    )(r2, ln_g.reshape(1, d), ln_b.reshape(1, d), modv3, modv3, wr_t, b_r.reshape(n_exp, 1))


def _start_row_gather(idx_ref, n_rows, src_ref, dst_ref, sem, unrolled):
    def start(r):
        pltpu.make_async_copy(src_ref.at[pl.ds(idx_ref[0, 0, r], 1)], dst_ref.at[pl.ds(r, 1)], sem).start()

    if unrolled:
        for r in range(n_rows):
            start(r)
    else:
        def body(r, carry):
            start(r)
            return carry
        lax.fori_loop(0, n_rows, body, 0, unroll=8)


def _wait_row_gather(dst_ref, sem):
    pltpu.make_async_copy(dst_ref, dst_ref, sem).wait()


GATHER_AHEAD = 2
GATHER_SLOTS = GATHER_AHEAD + 1


def _gather_index_specs(tm, n_tiles, index_of):
    assert n_tiles > GATHER_AHEAD
    return [pl.BlockSpec((1, 1, tm), index_of(a), memory_space=pltpu.SMEM) for a in range(GATHER_SLOTS)]


def _ring_gather_step(i, n_tiles, idx_refs, n_rows, src_ref, buf_ref, sems, ahead_unrolled):
    @pl.when(i == 0)
    def _():
        for a in range(GATHER_AHEAD):
            _start_row_gather(idx_refs[a], n_rows, src_ref, buf_ref.at[a], sems.at[a], False)

    slot = lax.rem(i, GATHER_SLOTS)
    ahead = lax.rem(i + GATHER_AHEAD, GATHER_SLOTS)
    _wait_row_gather(buf_ref.at[slot], sems.at[slot])

    def start_ahead(unrolled=ahead_unrolled):
        _start_row_gather(idx_refs[GATHER_AHEAD], n_rows, src_ref, buf_ref.at[ahead], sems.at[ahead], unrolled)

    return slot, start_ahead


def _drain_ring(i, n_tiles, buf_ref, sems):
    @pl.when(i == n_tiles - 1)
    def _():
        for a in range(1, GATHER_SLOTS):
            s = (n_tiles - 1 + a) % GATHER_SLOTS
            _wait_row_gather(buf_ref.at[s], sems.at[s])


def _moe_up_kernel(be_ref, nu_ref, idx0_ref, idx1_ref, idx2_ref, tok_ref, w_ref, bg_ref, bl_ref,
                   h_ref, xbuf, sems, *, tm, n_tiles, f):
    i = pl.program_id(0)
    slot, start_ahead = _ring_gather_step(i, n_tiles, (idx0_ref, idx1_ref, idx2_ref), tm, tok_ref,
                                          xbuf, sems, True)

    @pl.when(i < nu_ref[0])
    def _():
        start_ahead()
        x = xbuf[slot].astype(BF16)
        glu = jnp.minimum(_dot(x, w_ref[0, :, :f]) + bg_ref[0], SWIGLU_LIMIT)
        lin = jnp.clip(_dot(x, w_ref[0, :, f:]) + bl_ref[0], -SWIGLU_LIMIT, SWIGLU_LIMIT)
        h_ref[...] = (glu * jax.nn.sigmoid(SWIGLU_ALPHA * glu) * (lin + 1.0)).astype(BF16)

    @pl.when(i >= nu_ref[0])
    def _():
        start_ahead(False)
        h_ref[...] = jnp.zeros_like(h_ref)

    _drain_ring(i, n_tiles, xbuf, sems)


def _moe_up(tokens, row_idx, blk_exp, n_used, wgu_bf, bg, bl, tm):
    n = row_idx.shape[0]
    d = tokens.shape[1]
    n_exp, _, f2 = wgu_bf.shape
    f = f2 // 2
    n_tiles = n // tm
    idx3 = row_idx.reshape(n_tiles, 1, tm)
    kern = functools.partial(_moe_up_kernel, tm=tm, n_tiles=n_tiles, f=f)
    vmem = GATHER_SLOTS * tm * d * 4 + 2 * d * f2 * 2 + 2 * tm * f * 2 + 6 * tm * f * 4 + tm * d * 2 \
        + (4 << 20)
    idx_specs = _gather_index_specs(
        tm, n_tiles, lambda a: (lambda i, be, nu: (jnp.minimum(i + a, n_tiles - 1), 0, 0)))
    grid_spec = pltpu.PrefetchScalarGridSpec(
        num_scalar_prefetch=2,
        grid=(n_tiles,),
        in_specs=idx_specs + [
            pl.BlockSpec(memory_space=pl.ANY),
            pl.BlockSpec((1, d, f2), lambda i, be, nu: (be[i], 0, 0)),
            pl.BlockSpec((1, 1, f), lambda i, be, nu: (be[i], 0, 0)),
            pl.BlockSpec((1, 1, f), lambda i, be, nu: (be[i], 0, 0))],
        out_specs=pl.BlockSpec((tm, f), lambda i, be, nu: (i, 0)),
        scratch_shapes=[pltpu.VMEM((GATHER_SLOTS, tm, d), tokens.dtype),
                        pltpu.SemaphoreType.DMA((GATHER_SLOTS,))])
    return pl.pallas_call(
        kern,
        grid_spec=grid_spec,
        out_shape=jax.ShapeDtypeStruct((n, f), BF16),
        compiler_params=_params(("arbitrary",), vmem),
        name="moe_up",
    )(blk_exp, n_used, idx3, idx3, idx3, tokens, wgu_bf, bg.reshape(n_exp, 1, f), bl.reshape(n_exp, 1, f))


def _moe_down_kernel(be_ref, nu_ref, h_ref, w_ref, b_ref, y_ref):
    i = pl.program_id(0)

    @pl.when(i < nu_ref[0])
    def _():
        y_ref[...] = _dot(h_ref[...], w_ref[0]) + b_ref[0]

    @pl.when(i >= nu_ref[0])
    def _():
        y_ref[...] = jnp.zeros_like(y_ref)


def _moe_down(hs, blk_exp, n_used, wd_bf, bd, tm):
    n, f = hs.shape
    n_exp, _, d = wd_bf.shape
    vmem = 2 * (tm * f * 2 + f * d * 2 + tm * d * 4) + 2 * tm * d * 4 + (4 << 20)
    grid_spec = pltpu.PrefetchScalarGridSpec(
        num_scalar_prefetch=2,
        grid=(n // tm,),
        in_specs=[pl.BlockSpec((tm, f), lambda i, be, nu: (i, 0)),
                  pl.BlockSpec((1, f, d), lambda i, be, nu: (be[i], 0, 0)),
                  pl.BlockSpec((1, 1, d), lambda i, be, nu: (be[i], 0, 0))],
        out_specs=pl.BlockSpec((tm, d), lambda i, be, nu: (i, 0)))
    return pl.pallas_call(
        _moe_down_kernel,
        grid_spec=grid_spec,
        out_shape=jax.ShapeDtypeStruct((n, d), F32),
        compiler_params=_params(("arbitrary",), vmem),
        name="moe_down",
    )(blk_exp, n_used, hs, wd_bf, bd.reshape(n_exp, 1, d))


def _combine_kernel(pos0_ref, pos1_ref, pos2_ref, y_ref, gate_ref, x1_ref, g2_ref, lg_ref, lb_ref,
                    o_ref, buf_ref, sems, *, tm, n_steps):
    i = pl.program_id(0)
    slot, start_ahead = _ring_gather_step(i, n_steps, (pos0_ref, pos1_ref, pos2_ref), TOP_K * tm, y_ref,
                                          buf_ref, sems, True)
    start_ahead()
    gates = gate_ref[...]
    moe = gates[:, 0:1] * buf_ref[slot, 0:tm, :]
    for k in range(1, TOP_K):
        moe = moe + gates[:, k:k + 1] * buf_ref[slot, k * tm:(k + 1) * tm, :]
    pre = DN_ALPHA * x1_ref[...] + g2_ref[0] * moe
    o_ref[...] = _layer_norm_rows(pre, lg_ref[...], lb_ref[...])
    _drain_ring(i, n_steps, buf_ref, sems)


def _combine(y, pos, gates_t, x1, modv, ln_g, ln_b, rows_per_batch, g_col):
    r, d = x1.shape
    tm = _tile(rows_per_batch, 128)
    tiles_per_batch = rows_per_batch // tm
    n_steps = r // tm
    modv3 = modv.reshape(modv.shape[0], 1, modv.shape[1])
    pos_steps = pos.reshape(TOP_K, n_steps, tm).transpose(1, 0, 2).reshape(n_steps, 1, TOP_K * tm)
    kern = functools.partial(_combine_kernel, tm=tm, n_steps=n_steps)
    vmem = GATHER_SLOTS * TOP_K * tm * d * 4 + 4 * tm * d * 4 + 4 * tm * d * 4 + (4 << 20)
    pos_specs = _gather_index_specs(
        TOP_K * tm, n_steps, lambda a: (lambda i: (jnp.minimum(i + a, n_steps - 1), 0, 0)))
    return pl.pallas_call(
        kern,
        grid=(n_steps,),
        in_specs=pos_specs + [
            pl.BlockSpec(memory_space=pl.ANY),
            pl.BlockSpec((tm, TOP_K), lambda i: (i, 0)),
            pl.BlockSpec((tm, d), lambda i: (i, 0)),
            pl.BlockSpec((1, 1, d), lambda i: (i // tiles_per_batch, 0, g_col)),
            pl.BlockSpec((1, d), lambda i: (0, 0)),
            pl.BlockSpec((1, d), lambda i: (0, 0))],
        out_specs=pl.BlockSpec((tm, d), lambda i: (i, 0)),
        out_shape=jax.ShapeDtypeStruct((r, d), F32),
        scratch_shapes=[pltpu.VMEM((GATHER_SLOTS, TOP_K * tm, d), F32),
                        pltpu.SemaphoreType.DMA((GATHER_SLOTS,))],
        compiler_params=_params(("arbitrary",), vmem),
        name="combine",
    )(pos_steps, pos_steps, pos_steps, y, gates_t, x1, modv3, ln_g.reshape(1, d), ln_b.reshape(1, d))


def kernel(x, c, ctx, c_ctx, w_mod, b_mod, w_in, conv_w, conv_b, lru_wa, lru_ba, lru_wx, lru_bx,
           lru_lambda, sg_ln_g, sg_ln_b, sg_w, sg_b, w_branch_a, w_branch_b, w_out, ln1_g, ln1_b,
           w_router, b_router, w_gate_up, b_gate_up, w_down, b_down, ln2_g, ln2_b):
    assert w_mod.shape[0] == DEPTH
    batch, seq, d = x.shape
    ctx_len = ctx.shape[1]
    w_lru = conv_w.shape[-1]
    n_heads = lru_wa.shape[2]
    hd = w_lru // n_heads
    w_sg = sg_ln_g.shape[-1]
    n_exp = w_router.shape[-1]
    n_tok = batch * seq
    l = 0

    pad = (-(batch + 1)) % SUBLANES
    c_rows = jnp.concatenate([c, c_ctx[None, :], jnp.zeros((pad, d), F32)], axis=0)
    modv = _mod(c_rows, w_mod[l], b_mod[l])

    w_in_bf = w_in[l].astype(BF16)
    n_in = w_in_bf.shape[1]
    col_gy, col_u, col_v = w_lru, 2 * w_lru, 2 * w_lru + w_sg
    col_ga = 2 * w_lru + 2 * w_sg
    col_gb = col_ga + d
    x2 = x.reshape(n_tok, d)
    z = _inproj(x2, modv, w_in_bf, seq, 0, n_in, col_gy, col_ga)
    zc = _inproj(ctx.reshape(batch * ctx_len, d), modv, w_in_bf, batch * ctx_len, batch,
                 w_lru, w_lru, w_lru)

    wax = jnp.concatenate([lru_wa[l], lru_wx[l]], axis=-1).astype(BF16)
    y_a, wgu = _lru(z, zc, conv_w[l], conv_b[l], wax, lru_ba[l].reshape(2, 1, w_lru),
                    lru_bx[l].reshape(2, 1, w_lru), lru_lambda[l].reshape(2, 1, w_lru), w_gate_up[l],
                    batch, seq, ctx_len, w_lru, hd, col_gy)
    y_b = _spatial_gating(z, sg_ln_g[l], sg_ln_b[l], sg_w[l].astype(BF16), sg_b[l].T,
                          n_tok, w_sg, col_u, col_v)

    merged = _merge(y_a, y_b, w_branch_a[l].astype(BF16), w_branch_b[l].astype(BF16), z, col_ga, col_gb)
    r1 = _outproj(merged, w_out[l].astype(BF16), x2, modv, seq, 2)
    x1, tokens, idx_t, gate_t, rank_t, counts = _ln_route(
        r1, ln1_g[l], ln1_b[l], modv, w_router[l].T, b_router[l], seq, 4, 3)

    tm_moe = 256
    n_assign = n_tok * TOP_K
    n_tiles = -(-(n_assign + n_exp * (tm_moe - 1)) // tm_moe)
    n_pad = n_tiles * tm_moe
    cnt = counts[:, 0]
    padded = (cnt + tm_moe - 1) // tm_moe * tm_moe
    pad_end = jnp.cumsum(padded)
    pad_start = pad_end - padded
    expert_ids = jnp.arange(n_exp, dtype=jnp.int32)
    pos = rank_t + jnp.sum(jnp.where(idx_t[..., None] == expert_ids, pad_start, 0), axis=-1)
    tok_ids = jnp.broadcast_to(jnp.arange(n_tok, dtype=jnp.int32)[None, :], (TOP_K, n_tok))
    tok_pad = jnp.zeros((n_pad,), jnp.int32).at[pos.reshape(-1)].set(tok_ids.reshape(-1))
    blk_start = jnp.arange(n_tiles, dtype=jnp.int32) * tm_moe
    blk_exp = jnp.minimum(jnp.sum(blk_start[:, None] >= pad_end[None, :], axis=1), n_exp - 1).astype(jnp.int32)
    n_used = (pad_end[-1] // tm_moe).astype(jnp.int32).reshape(1)

    hs = _moe_up(tokens, tok_pad, blk_exp, n_used, wgu,
                 b_gate_up[l][..., ::2], b_gate_up[l][..., 1::2], tm_moe)
    ys = _moe_down(hs, blk_exp, n_used, w_down[l].astype(BF16), b_down[l], tm_moe)

    out = _combine(ys, pos, gate_t.T, x1, modv, ln2_g[l], ln2_b[l], seq, 5)
    return out.reshape(batch, seq, d)
```

```python
import functools

import jax
import jax.numpy as jnp
from jax import lax
from jax.experimental import pallas as pl
from jax.experimental.pallas import tpu as pltpu

TOP_K = 4
LRU_C = 8.0
CONV_WIDTH = 4
CONV_PAD_LEFT = 1
SWIGLU_LIMIT = 7.0
SWIGLU_ALPHA = 1.702
DEPTH = 1
DN_ALPHA = (2.0 * DEPTH) ** 0.25
LN_EPS = 1e-5
N_MOD = 6

SUBLANES = 8
LANES = 128
V7X_VMEM_CAP = 60000 * 1024

F32 = jnp.float32
BF16 = jnp.bfloat16


def _params(semantics, vmem_bytes):
    return pltpu.CompilerParams(dimension_semantics=semantics,
                                vmem_limit_bytes=int(min(vmem_bytes, V7X_VMEM_CAP)))


def _tile(n, pref):
    t = min(n, pref)
    while n % t:
        t //= 2
    return t


def _dot(a, b):
    return jnp.dot(a, b, preferred_element_type=F32)


def _layer_norm_rows(x, g, b):
    mu = jnp.mean(x, axis=-1, keepdims=True)
    xc = x - mu
    var = jnp.mean(xc * xc, axis=-1, keepdims=True)
    return xc * lax.rsqrt(var + LN_EPS) * g + b


def _mod_kernel(c_ref, w_ref, b_ref, o_ref):
    c = c_ref[...]
    s = c * jax.nn.sigmoid(c)
    o_ref[...] = _dot(s.astype(BF16), w_ref[...].astype(BF16)) + b_ref[...]


def _mod(c_rows, w_mod, b_mod):
    m, d = c_rows.shape
    n = w_mod.shape[1]
    tn = _tile(n, 512)
    return pl.pallas_call(
        _mod_kernel,
        grid=(n // tn,),
        in_specs=[pl.BlockSpec((m, d), lambda j: (0, 0)),
                  pl.BlockSpec((d, tn), lambda j: (0, j)),
                  pl.BlockSpec((1, tn), lambda j: (0, j))],
        out_specs=pl.BlockSpec((m, tn), lambda j: (0, j)),
        out_shape=jax.ShapeDtypeStruct((m, n), F32),
        compiler_params=_params(("arbitrary",), 3 * d * tn * 4 + (4 << 20)),
        name="mod",
    )(c_rows, w_mod, b_mod.reshape(1, n))


def _inproj_kernel(x_ref, sc_ref, sh_ref, w_ref, o_ref, h_ref, *, n_plain, n_gelu):
    j = pl.program_id(1)

    @pl.when(j == 0)
    def _():
        h = x_ref[...] * (1.0 + sc_ref[0]) + sh_ref[0]
        h_ref[...] = h.astype(BF16)

    @pl.when(j < n_plain)
    def _():
        o_ref[...] = _dot(h_ref[...], w_ref[...])

    @pl.when(jnp.logical_and(j >= n_plain, j < n_gelu))
    def _():
        o_ref[...] = jax.nn.gelu(_dot(h_ref[...], w_ref[...]))

    @pl.when(j >= n_gelu)
    def _():
        o_ref[...] = jax.nn.sigmoid(_dot(h_ref[...], w_ref[...]))


def _inproj(x2, modv, w_bf, rows_per_batch, mod_row0, n_cols, plain_cols, gelu_cols):
    r, d = x2.shape
    tm = _tile(rows_per_batch, 512)
    tn = _tile(_gcd_all(n_cols, plain_cols, gelu_cols), 1024)
    tiles_per_batch = rows_per_batch // tm
    modv3 = modv.reshape(modv.shape[0], 1, modv.shape[1])

    def mod_map(col):
        return lambda i, j: (mod_row0 + i // tiles_per_batch, 0, col)

    kern = functools.partial(_inproj_kernel, n_plain=plain_cols // tn, n_gelu=gelu_cols // tn)
    vmem = 2 * tm * d * 4 + tm * d * 2 + 2 * d * tn * 2 + 4 * tm * tn * 4 + (4 << 20)
    return pl.pallas_call(
        kern,
        grid=(r // tm, n_cols // tn),
        in_specs=[pl.BlockSpec((tm, d), lambda i, j: (i, 0)),
                  pl.BlockSpec((1, 1, d), mod_map(1)),
                  pl.BlockSpec((1, 1, d), mod_map(0)),
                  pl.BlockSpec((d, tn), lambda i, j: (0, j))],
        out_specs=pl.BlockSpec((tm, tn), lambda i, j: (i, j)),
        out_shape=jax.ShapeDtypeStruct((r, n_cols), F32),
        scratch_shapes=[pltpu.VMEM((tm, d), BF16)],
        compiler_params=_params(("parallel", "arbitrary"), vmem),
        name="inproj",
    )(x2, modv3, modv3, w_bf)


def _gcd_all(*vals):
    import math
    g = 0
    for v in vals:
        g = math.gcd(g, v)
    return g


def _deinterleave_perm(grp):
    half = grp // 2
    s_idx = lax.broadcasted_iota(jnp.int32, (grp, grp), 0)
    t_idx = lax.broadcasted_iota(jnp.int32, (grp, grp), 1)
    src = jnp.where(t_idx < half, 2 * t_idx, 2 * (t_idx - half) + 1)
    return jnp.where(s_idx == src, 1.0, 0.0).astype(BF16)


def _deinterleave_rows(w_rows, perm, f, store):
    grp = perm.shape[0]
    half = grp // 2
    for g in range(2 * f // grp):
        p = _dot(w_rows[:, g * grp:(g + 1) * grp].astype(BF16), perm)
        store(g * half, p[:, :half].astype(BF16))
        store(f + g * half, p[:, half:].astype(BF16))


PREP_IN_SLOTS = 4
PREP_OUT_SLOTS = 2


class _WeightPrepStream:
    def __init__(self, src_ref, dst_ref, win_ref, wout_ref, sems_in, sems_out, row0, chunk_rows, n_chunks,
                 deinterleave):
        self.src, self.dst, self.win, self.wout = src_ref, dst_ref, win_ref, wout_ref
        self.sems_in, self.sems_out = sems_in, sems_out
        self.row0, self.cr, self.n = row0, chunk_rows, n_chunks
        self.f = src_ref.shape[1] // 2
        self.perm = _deinterleave_perm(min(src_ref.shape[1], 2 * LANES)) if deinterleave else None

    def _rows(self, c):
        return pl.ds(pl.multiple_of(self.row0 + c * self.cr, self.cr), self.cr)

    def _in_copy(self, c, slot):
        return pltpu.make_async_copy(self.src.at[self._rows(c)], self.win.at[slot], self.sems_in.at[slot])

    def _out_copy(self, c, slot):
        return pltpu.make_async_copy(self.wout.at[slot], self.dst.at[self._rows(c)], self.sems_out.at[slot])

    @property
    def _n_in(self):
        return self.win.shape[0]

    @property
    def _n_out(self):
        return self.wout.shape[0]

    def prologue(self):
        for c in range(min(self._n_in, self.n)):
            self._in_copy(c, c).start()

    def begin(self, c):
        out_slot = lax.rem(c, self._n_out)

        @pl.when(c >= self._n_out)
        def _():
            self._out_copy(c - self._n_out, out_slot).wait()

        self._in_copy(c, lax.rem(c, self._n_in)).wait()

    def convert(self, c):
        in_slot = lax.rem(c, self._n_in)
        out_slot = lax.rem(c, self._n_out)

        def store(col0, block):
            self.wout[out_slot, :, col0:col0 + block.shape[1]] = block

        if self.perm is None:
            self.wout[out_slot] = self.win[in_slot].astype(BF16)
        else:
            _deinterleave_rows(self.win[in_slot], self.perm, self.f, store)

    def end(self, c):
        self._out_copy(c, lax.rem(c, self._n_out)).start()

        @pl.when(c + self._n_in < self.n)
        def _():
            self._in_copy(c + self._n_in, lax.rem(c, self._n_in)).start()

    def epilogue(self):
        for c in range(max(self.n - self._n_out, 0), self.n):
            self._out_copy(c, c % self._n_out).wait()


def _lru_kernel(zr_ref, gy_ref, zc_ref, cw_ref, cb_ref, wax_ref, ba_ref, bx_ref, lam_ref, wgu_ref,
                o_ref, wgu_out_ref, hf_ref, hb_ref, win_ref, wout_ref, sems_in, sems_out,
                *, seq, ctx_len, rows, cb, hd, prep_rows):
    n_heads = cb // hd
    groups = rows // SUBLANES
    cw = cw_ref[...]
    cbias = cb_ref[...]
    row_in_group = lax.broadcasted_iota(jnp.int32, (groups, SUBLANES, cb), 1)
    n_chunks = seq // rows
    step_id = pl.program_id(0) * pl.num_programs(1) + pl.program_id(1)
    prep = _WeightPrepStream(wgu_ref, wgu_out_ref, win_ref, wout_ref, sems_in, sems_out,
                             step_id * (n_chunks * prep_rows), prep_rows, n_chunks, True)

    def conv_chunk(ref, r0, total):
        p0 = jnp.maximum(r0 - SUBLANES, 0)
        prev = jnp.where(r0 > 0, ref[pl.ds(pl.multiple_of(p0, SUBLANES), SUBLANES), :], 0.0)
        n0 = jnp.minimum(r0 + rows, total - SUBLANES)
        nxt = jnp.where(r0 + rows < total, ref[pl.ds(pl.multiple_of(n0, SUBLANES), SUBLANES), :], 0.0)
        cur = ref[pl.ds(r0, rows), :]
        ext = jnp.concatenate([prev, cur, nxt], axis=0)
        n_ext = rows + 2 * SUBLANES
        u = cbias + cw[1:2] * cur
        u = u + cw[0:1] * pltpu.roll(ext, 1, axis=0)[SUBLANES:SUBLANES + rows]
        u = u + cw[2:3] * pltpu.roll(ext, n_ext - 1, axis=0)[SUBLANES:SUBLANES + rows]
        u = u + cw[3:4] * pltpu.roll(ext, n_ext - 2, axis=0)[SUBLANES:SUBLANES + rows]
        return u

    def gates(u, direction):
        ub = u.astype(BF16)
        pre_a, pre_x = [], []
        for h in range(n_heads):
            g = _dot(ub[:, h * hd:(h + 1) * hd], wax_ref[direction, h])
            pre_a.append(g[:, :hd])
            pre_x.append(g[:, hd:])
        pre_a = pre_a[0] if n_heads == 1 else jnp.concatenate(pre_a, axis=1)
        pre_x = pre_x[0] if n_heads == 1 else jnp.concatenate(pre_x, axis=1)
        r = jax.nn.sigmoid(pre_a + ba_ref[direction])
        i = jax.nn.sigmoid(pre_x + bx_ref[direction])
        neg_lam = -lam_ref[direction]
        softplus = jnp.maximum(neg_lam, 0.0) + jnp.log1p(jnp.exp(-jnp.abs(neg_lam)))
        log_a = -LRU_C * r * softplus
        a = jnp.exp(log_a)
        q = jnp.tanh(-log_a) * (1.0 + a * a)
        mult = jnp.where(q > 0.0, q * lax.rsqrt(q), 0.0)
        return a, mult * i * u

    def local_scan(a, b, reverse):
        a = a.reshape(groups, SUBLANES, cb)
        b = b.reshape(groups, SUBLANES, cb)
        d = 1
        while d < SUBLANES:
            if reverse:
                shift, ok = SUBLANES - d, row_in_group < SUBLANES - d
            else:
                shift, ok = d, row_in_group >= d
            a_s = pltpu.roll(a, shift, axis=1)
            b_s = pltpu.roll(b, shift, axis=1)
            b = b + jnp.where(ok, a * b_s, 0.0)
            a = jnp.where(ok, a * a_s, a)
            d *= 2
        return a.reshape(rows, cb), b.reshape(rows, cb)

    def scan_chunk(ref, r0, total, direction, reverse, h, dst_ref):
        u = conv_chunk(ref, r0, total)
        a, b = gates(u, direction)
        a, b = local_scan(a, b, reverse)
        order = range(groups - 1, -1, -1) if reverse else range(groups)
        for g in order:
            sl = slice(g * SUBLANES, (g + 1) * SUBLANES)
            hg = b[sl] + a[sl] * h
            if dst_ref is not None:
                dst_ref[pl.ds(pl.multiple_of(r0 + g * SUBLANES, SUBLANES), SUBLANES), :] = hg
            h = hg[0:1] if reverse else hg[SUBLANES - 1:SUBLANES]
        return h

    def sweep(ref, total, h_fwd, h_bwd, fwd_dst, bwd_dst, side_work=None):
        n = total // rows

        def body(c, carry):
            hf, hb = carry
            if side_work is not None:
                side_work.begin(c)
                side_work.convert(c)
            hf = scan_chunk(ref, pl.multiple_of(c * rows, rows), total, 0, False, hf, fwd_dst)
            hb = scan_chunk(ref, pl.multiple_of((n - 1 - c) * rows, rows), total, 1, True, hb, bwd_dst)
            if side_work is not None:
                side_work.end(c)
            return hf, hb

        return lax.fori_loop(0, n, body, (h_fwd, h_bwd))

    zero = jnp.zeros((1, cb), F32)
    prep.prologue()
    h_f, h_b = sweep(zc_ref, ctx_len, zero, zero, None, None)
    sweep(zr_ref, seq, h_f, h_b, hf_ref, hb_ref, prep)
    prep.epilogue()

    def out_body(c, carry):
        sl = pl.ds(pl.multiple_of(c * rows, rows), rows)
        o_ref[sl, :] = ((hf_ref[sl, :] + hb_ref[sl, :]) * gy_ref[sl, :]).astype(BF16)
        return carry

    lax.fori_loop(0, seq // rows, out_body, 0)


def _lru(z, zc, conv_w, conv_b, wax, ba, bx, lam, w_gate_up, batch, seq, ctx_len, w_lru, hd, gy_col0):
    cb = _tile(w_lru, 2 * hd)
    rows = _tile(_gcd_all(seq, ctx_len), 256)
    n_cb = w_lru // cb
    n_exp, d, f2 = w_gate_up.shape
    wgu_rows = w_gate_up.reshape(n_exp * d, f2)
    n_prep_chunks = batch * n_cb * (seq // rows)
    assert (n_exp * d) % n_prep_chunks == 0
    prep_rows = (n_exp * d) // n_prep_chunks
    assert prep_rows % (2 * SUBLANES) == 0
    kern = functools.partial(_lru_kernel, seq=seq, ctx_len=ctx_len, rows=rows, cb=cb, hd=hd,
                             prep_rows=prep_rows)
    vmem = 4 * seq * cb * 4 + 2 * seq * cb * 2 + 2 * seq * cb * 4 + 80 * rows * cb * 4 \
        + PREP_IN_SLOTS * prep_rows * f2 * 4 + PREP_OUT_SLOTS * prep_rows * f2 * 2 + 4 * prep_rows * f2 * 4 \
        + (4 << 20)
    y_a, wgu_bf = pl.pallas_call(
        kern,
        grid=(batch, n_cb),
        in_specs=[pl.BlockSpec((seq, cb), lambda b, j: (b, j)),
                  pl.BlockSpec((seq, cb), lambda b, j: (b, gy_col0 // cb + j)),
                  pl.BlockSpec((ctx_len, cb), lambda b, j: (b, j)),
                  pl.BlockSpec((CONV_WIDTH, cb), lambda b, j: (0, j)),
                  pl.BlockSpec((1, cb), lambda b, j: (0, j)),
                  pl.BlockSpec((2, cb // hd, hd, 2 * hd), lambda b, j: (0, j, 0, 0)),
                  pl.BlockSpec((2, 1, cb), lambda b, j: (0, 0, j)),
                  pl.BlockSpec((2, 1, cb), lambda b, j: (0, 0, j)),
                  pl.BlockSpec((2, 1, cb), lambda b, j: (0, 0, j)),
                  pl.BlockSpec(memory_space=pl.ANY)],
        out_specs=[pl.BlockSpec((seq, cb), lambda b, j: (b, j)),
                   pl.BlockSpec(memory_space=pl.ANY)],
        out_shape=[jax.ShapeDtypeStruct((batch * seq, w_lru), BF16),
                   jax.ShapeDtypeStruct((n_exp * d, f2), BF16)],
        scratch_shapes=[pltpu.VMEM((seq, cb), F32), pltpu.VMEM((seq, cb), F32),
                        pltpu.VMEM((PREP_IN_SLOTS, prep_rows, f2), F32),
                        pltpu.VMEM((PREP_OUT_SLOTS, prep_rows, f2), BF16),
                        pltpu.SemaphoreType.DMA((PREP_IN_SLOTS,)), pltpu.SemaphoreType.DMA((PREP_OUT_SLOTS,))],
        compiler_params=_params(("arbitrary", "arbitrary"), vmem),
        name="lru",
    )(z, z, zc, conv_w, conv_b.reshape(1, w_lru), wax, ba, bx, lam, wgu_rows)
    return y_a, wgu_bf.reshape(n_exp, d, f2)


def _sg_kernel(u_ref, v_ref, g_ref, b_ref, sw_ref, sbt_ref, o_ref, *, n_sub, chunk, n_groups, gd):
    vn = _layer_norm_rows(v_ref[...], g_ref[...], b_ref[...]).astype(BF16)
    for c in range(n_sub):
        rs = slice(c * chunk, (c + 1) * chunk)
        for g in range(n_groups):
            cs = slice(g * gd, (g + 1) * gd)
            mixed = _dot(sw_ref[g], vn[rs, cs]) + sbt_ref[:, g:g + 1]
            o_ref[rs, cs] = (u_ref[rs, cs] * mixed).astype(BF16)


def _spatial_gating(z, ln_g, ln_b, sw_bf, sb_t, n_rows, w_sg, u_col0, v_col0):
    n_groups, chunk, _ = sw_bf.shape
    gd = w_sg // n_groups
    n_sub = _tile(n_rows // chunk, 4)
    tm = n_sub * chunk
    kern = functools.partial(_sg_kernel, n_sub=n_sub, chunk=chunk, n_groups=n_groups, gd=gd)
    vmem = 4 * tm * w_sg * 4 + 2 * tm * w_sg * 2 + 6 * tm * w_sg * 4 + (4 << 20)
    return pl.pallas_call(
        kern,
        grid=(n_rows // tm,),
        in_specs=[pl.BlockSpec((tm, w_sg), lambda i: (i, u_col0 // w_sg)),
                  pl.BlockSpec((tm, w_sg), lambda i: (i, v_col0 // w_sg)),
                  pl.BlockSpec((1, w_sg), lambda i: (0, 0)),
                  pl.BlockSpec((1, w_sg), lambda i: (0, 0)),
                  pl.BlockSpec((n_groups, chunk, chunk), lambda i: (0, 0, 0)),
                  pl.BlockSpec((chunk, n_groups), lambda i: (0, 0))],
        out_specs=pl.BlockSpec((tm, w_sg), lambda i: (i, 0)),
        out_shape=jax.ShapeDtypeStruct((n_rows, w_sg), BF16),
        compiler_params=_params(("parallel",), vmem),
        name="sg",
    )(z, z, ln_g.reshape(1, w_sg), ln_b.reshape(1, w_sg), sw_bf, sb_t)


def _merge_kernel(ya_ref, yb_ref, wa_ref, wb_ref, ga_ref, gb_ref, o_ref):
    pa = _dot(ya_ref[...], wa_ref[...])
    pb = _dot(yb_ref[...], wb_ref[...])
    o_ref[...] = (ga_ref[...] * pa + gb_ref[...] * pb).astype(BF16)


def _merge(ya, yb, wa_bf, wb_bf, z, ga_col0, gb_col0):
    r, ka = ya.shape
    kb = yb.shape[1]
    d = wa_bf.shape[1]
    tm = _tile(r, 1024)
    tn = _tile(d, 512)
    vmem = 2 * (tm * (ka + kb) * 2 + (ka + kb) * tn * 2 + 2 * tm * tn * 4 + tm * tn * 2) \
        + 4 * tm * tn * 4 + (4 << 20)
    return pl.pallas_call(
        _merge_kernel,
        grid=(r // tm, d // tn),
        in_specs=[pl.BlockSpec((tm, ka), lambda i, j: (i, 0)),
                  pl.BlockSpec((tm, kb), lambda i, j: (i, 0)),
                  pl.BlockSpec((ka, tn), lambda i, j: (0, j)),
                  pl.BlockSpec((kb, tn), lambda i, j: (0, j)),
                  pl.BlockSpec((tm, tn), lambda i, j: (i, ga_col0 // tn + j)),
                  pl.BlockSpec((tm, tn), lambda i, j: (i, gb_col0 // tn + j))],
        out_specs=pl.BlockSpec((tm, tn), lambda i, j: (i, j)),
        out_shape=jax.ShapeDtypeStruct((r, d), BF16),
        compiler_params=_params(("parallel", "parallel"), vmem),
        name="merge",
    )(ya, yb, wa_bf, wb_bf, z, z)


def _outproj_kernel(m_ref, w_ref, x_ref, g_ref, o_ref):
    o_ref[...] = DN_ALPHA * x_ref[...] + g_ref[0] * _dot(m_ref[...], w_ref[...])


def _outproj(merged, wo_bf, x2, modv, rows_per_batch, g_col):
    r, k = merged.shape
    d = wo_bf.shape[1]
    tm = _tile(rows_per_batch, 1024)
    tn = _tile(d, 1024)
    tiles_per_batch = rows_per_batch // tm
    modv3 = modv.reshape(modv.shape[0], 1, modv.shape[1])
    vmem = 2 * (tm * k * 2 + k * tn * 2 + 2 * tm * tn * 4) + 2 * tm * tn * 4 + (4 << 20)
    return pl.pallas_call(
        _outproj_kernel,
        grid=(r // tm, d // tn),
        in_specs=[pl.BlockSpec((tm, k), lambda i, j: (i, 0)),
                  pl.BlockSpec((k, tn), lambda i, j: (0, j)),
                  pl.BlockSpec((tm, tn), lambda i, j: (i, j)),
                  pl.BlockSpec((1, 1, tn), lambda i, j: (i // tiles_per_batch, 0, g_col * (d // tn) + j))],
        out_specs=pl.BlockSpec((tm, tn), lambda i, j: (i, j)),
        out_shape=jax.ShapeDtypeStruct((r, d), F32),
        compiler_params=_params(("parallel", "parallel"), vmem),
        name="outproj",
    )(merged, wo_bf, x2, modv3)


def _ln_route_kernel(r_ref, g_ref, b_ref, sc_ref, sh_ref, wrt_ref, br_ref,
                     x1_ref, tok_ref, idx_ref, gate_ref, rank_ref, cnt_ref, run_ref, *, n_exp):
    i = pl.program_id(0)
    tm = r_ref.shape[0]

    @pl.when(i == 0)
    def _():
        run_ref[...] = jnp.zeros_like(run_ref)

    x1 = _layer_norm_rows(r_ref[...], g_ref[...], b_ref[...])
    x1_ref[...] = x1
    tok = x1 * (1.0 + sc_ref[0]) + sh_ref[0]
    tok_ref[...] = tok

    logits = lax.dot_general(wrt_ref[...], tok, (((1,), (1,)), ((), ())),
                             precision=lax.Precision.HIGHEST,
                             preferred_element_type=F32) + br_ref[...]
    eidx = lax.broadcasted_iota(jnp.int32, (n_exp, tm), 0)
    vals, idxs = [], []
    work = logits
    for _ in range(TOP_K):
        m = jnp.max(work, axis=0, keepdims=True)
        idx = jnp.min(jnp.where(work == m, eidx, n_exp), axis=0, keepdims=True)
        vals.append(m)
        idxs.append(idx)
        work = jnp.where(eidx == idx, -jnp.inf, work)
    exps = [jnp.exp(v - vals[0]) for v in vals]
    denom = exps[0]
    for e in exps[1:]:
        denom = denom + e
    gate_ref[...] = jnp.concatenate([e / denom for e in exps], axis=0)
    idx_ref[...] = jnp.concatenate(idxs, axis=0)

    sel = [eidx == idx for idx in idxs]
    onehot = sel[0]
    for s in sel[1:]:
        onehot = jnp.logical_or(onehot, s)
    onehot = jnp.where(onehot, 1.0, 0.0)
    s_idx = lax.broadcasted_iota(jnp.int32, (tm, tm), 0)
    t_idx = lax.broadcasted_iota(jnp.int32, (tm, tm), 1)
    strict_upper = jnp.where(s_idx < t_idx, 1.0, 0.0).astype(BF16)
    before = _dot(onehot.astype(BF16), strict_upper) + run_ref[:, 0:1]
    ranks = [jnp.sum(jnp.where(s, before, 0.0), axis=0, keepdims=True) for s in sel]
    rank_ref[...] = jnp.concatenate(ranks, axis=0).astype(jnp.int32)
    run_ref[...] = run_ref[...] + jnp.sum(onehot, axis=1, keepdims=True)
    cnt_ref[...] = run_ref[...].astype(jnp.int32)


def _ln_route(r2, ln_g, ln_b, modv, wr_t, b_r, rows_per_batch, sc_col, sh_col):
    r, d = r2.shape
    n_exp = wr_t.shape[0]
    tm = _tile(rows_per_batch, 256)
    tiles_per_batch = rows_per_batch // tm
    modv3 = modv.reshape(modv.shape[0], 1, modv.shape[1])
    kern = functools.partial(_ln_route_kernel, n_exp=n_exp)
    vmem = 6 * tm * d * 4 + 6 * tm * d * 4 + (8 << 20)
    small = lambda dt: jax.ShapeDtypeStruct((TOP_K, r), dt)
    return pl.pallas_call(
        kern,
        grid=(r // tm,),
        in_specs=[pl.BlockSpec((tm, d), lambda i: (i, 0)),
                  pl.BlockSpec((1, d), lambda i: (0, 0)),
                  pl.BlockSpec((1, d), lambda i: (0, 0)),
                  pl.BlockSpec((1, 1, d), lambda i: (i // tiles_per_batch, 0, sc_col)),
                  pl.BlockSpec((1, 1, d), lambda i: (i // tiles_per_batch, 0, sh_col)),
                  pl.BlockSpec((n_exp, d), lambda i: (0, 0)),
                  pl.BlockSpec((n_exp, 1), lambda i: (0, 0))],
        out_specs=[pl.BlockSpec((tm, d), lambda i: (i, 0)),
                   pl.BlockSpec((tm, d), lambda i: (i, 0)),
                   pl.BlockSpec((TOP_K, tm), lambda i: (0, i)),
                   pl.BlockSpec((TOP_K, tm), lambda i: (0, i)),
                   pl.BlockSpec((TOP_K, tm), lambda i: (0, i)),
                   pl.BlockSpec((n_exp, LANES), lambda i: (0, 0))],
        out_shape=[jax.ShapeDtypeStruct((r, d), F32),
                   jax.ShapeDtypeStruct((r, d), F32),
                   small(jnp.int32), small(F32), small(jnp.int32),
                   jax.ShapeDtypeStruct((n_exp, LANES), jnp.int32)],
        scratch_shapes=[pltpu.VMEM((n_exp, LANES), F32)],
        compiler_params=_params(("arbitrary",), vmem),
        name="ln_route",
    )(r2, ln_g.reshape(1, d), ln_b.reshape(1, d), modv3, modv3, wr_t, b_r.reshape(n_exp, 1))


def _start_row_gather(idx_ref, n_rows, src_ref, dst_ref, sem, unrolled):
    def start(r):
        pltpu.make_async_copy(src_ref.at[pl.ds(idx_ref[0, 0, r], 1)], dst_ref.at[pl.ds(r, 1)], sem).start()

    if unrolled:
        for r in range(n_rows):
            start(r)
    else:
        def body(r, carry):
            start(r)
            return carry
        lax.fori_loop(0, n_rows, body, 0, unroll=8)


def _wait_row_gather(dst_ref, sem):
    pltpu.make_async_copy(dst_ref, dst_ref, sem).wait()


GATHER_AHEAD = 2
GATHER_SLOTS = GATHER_AHEAD + 1


def _gather_index_specs(tm, n_tiles, index_of):
    assert n_tiles > GATHER_AHEAD
    return [pl.BlockSpec((1, 1, tm), index_of(a), memory_space=pltpu.SMEM) for a in range(GATHER_SLOTS)]


def _ring_gather_step(i, n_tiles, idx_refs, n_rows, src_ref, buf_ref, sems, ahead_unrolled):
    @pl.when(i == 0)
    def _():
        for a in range(GATHER_AHEAD):
            _start_row_gather(idx_refs[a], n_rows, src_ref, buf_ref.at[a], sems.at[a], False)

    slot = lax.rem(i, GATHER_SLOTS)
    ahead = lax.rem(i + GATHER_AHEAD, GATHER_SLOTS)
    _wait_row_gather(buf_ref.at[slot], sems.at[slot])

    def start_ahead(unrolled=ahead_unrolled):
        _start_row_gather(idx_refs[GATHER_AHEAD], n_rows, src_ref, buf_ref.at[ahead], sems.at[ahead], unrolled)

    return slot, start_ahead


def _drain_ring(i, n_tiles, buf_ref, sems):
    @pl.when(i == n_tiles - 1)
    def _():
        for a in range(1, GATHER_SLOTS):
            s = (n_tiles - 1 + a) % GATHER_SLOTS
            _wait_row_gather(buf_ref.at[s], sems.at[s])


def _moe_up_kernel(be_ref, nu_ref, idx0_ref, idx1_ref, idx2_ref, tok_ref, w_ref, bg_ref, bl_ref, wd_ref,
                   h_ref, wd_out_ref, xbuf, sems, win_ref, wout_ref, sems_in, sems_out,
                   *, tm, n_tiles, f, n_prep, prep_rows):
    i = pl.program_id(0)
    slot, start_ahead = _ring_gather_step(i, n_tiles, (idx0_ref, idx1_ref, idx2_ref), tm, tok_ref,
                                          xbuf, sems, True)
    prep = _WeightPrepStream(wd_ref, wd_out_ref, win_ref, wout_ref, sems_in, sems_out, 0, prep_rows, n_prep,
                             False)
    pl.when(i == 0)(prep.prologue)
    pl.when(i < n_prep)(functools.partial(prep.begin, i))

    def compute(with_prep):
        start_ahead()
        if with_prep:
            prep.convert(i)
        x = xbuf[slot].astype(BF16)
        glu = jnp.minimum(_dot(x, w_ref[0, :, :f]) + bg_ref[0], SWIGLU_LIMIT)
        lin = jnp.clip(_dot(x, w_ref[0, :, f:]) + bl_ref[0], -SWIGLU_LIMIT, SWIGLU_LIMIT)
        h_ref[...] = (glu * jax.nn.sigmoid(SWIGLU_ALPHA * glu) * (lin + 1.0)).astype(BF16)

    pl.when(i < n_prep)(functools.partial(compute, True))
    pl.when(jnp.logical_and(i >= n_prep, i < nu_ref[0]))(functools.partial(compute, False))

    @pl.when(i >= nu_ref[0])
    def _():
        start_ahead(False)
        h_ref[...] = jnp.zeros_like(h_ref)

    pl.when(i < n_prep)(functools.partial(prep.end, i))
    pl.when(i == n_prep)(prep.epilogue)
    _drain_ring(i, n_tiles, xbuf, sems)


MOE_PREP_SLOTS = 2


def _moe_up(tokens, row_idx, blk_exp, n_used, wgu_bf, bg, bl, w_down, min_used_tiles, tm):
    n = row_idx.shape[0]
    d = tokens.shape[1]
    n_exp, _, f2 = wgu_bf.shape
    f = f2 // 2
    n_tiles = n // tm
    assert min_used_tiles < n_tiles and (n_exp * f) % min_used_tiles == 0
    prep_rows = (n_exp * f) // min_used_tiles
    assert prep_rows % (2 * SUBLANES) == 0
    wd_rows = w_down.reshape(n_exp * f, d)
    idx3 = row_idx.reshape(n_tiles, 1, tm)
    kern = functools.partial(_moe_up_kernel, tm=tm, n_tiles=n_tiles, f=f, n_prep=min_used_tiles,
                             prep_rows=prep_rows)
    vmem = GATHER_SLOTS * tm * d * 4 + 2 * d * f2 * 2 + 2 * tm * f * 2 + 6 * tm * f * 4 + tm * d * 2 \
        + MOE_PREP_SLOTS * prep_rows * d * (4 + 2) + (4 << 20)
    idx_specs = _gather_index_specs(
        tm, n_tiles, lambda a: (lambda i, be, nu: (jnp.minimum(i + a, n_tiles - 1), 0, 0)))
    grid_spec = pltpu.PrefetchScalarGridSpec(
        num_scalar_prefetch=2,
        grid=(n_tiles,),
        in_specs=idx_specs + [
            pl.BlockSpec(memory_space=pl.ANY),
            pl.BlockSpec((1, d, f2), lambda i, be, nu: (be[i], 0, 0)),
            pl.BlockSpec((1, 1, f), lambda i, be, nu: (be[i], 0, 0)),
            pl.BlockSpec((1, 1, f), lambda i, be, nu: (be[i], 0, 0)),
            pl.BlockSpec(memory_space=pl.ANY)],
        out_specs=[pl.BlockSpec((tm, f), lambda i, be, nu: (i, 0)),
                   pl.BlockSpec(memory_space=pl.ANY)],
        scratch_shapes=[pltpu.VMEM((GATHER_SLOTS, tm, d), tokens.dtype),
                        pltpu.SemaphoreType.DMA((GATHER_SLOTS,)),
                        pltpu.VMEM((MOE_PREP_SLOTS, prep_rows, d), F32),
                        pltpu.VMEM((MOE_PREP_SLOTS, prep_rows, d), BF16),
                        pltpu.SemaphoreType.DMA((MOE_PREP_SLOTS,)),
                        pltpu.SemaphoreType.DMA((MOE_PREP_SLOTS,))])
    hs, wd_bf = pl.pallas_call(
        kern,
        grid_spec=grid_spec,
        out_shape=[jax.ShapeDtypeStruct((n, f), BF16), jax.ShapeDtypeStruct((n_exp * f, d), BF16)],
        compiler_params=_params(("arbitrary",), vmem),
        name="moe_up",
    )(blk_exp, n_used, idx3, idx3, idx3, tokens, wgu_bf, bg.reshape(n_exp, 1, f), bl.reshape(n_exp, 1, f),
      wd_rows)
    return hs, wd_bf.reshape(n_exp, f, d)


def _moe_down_kernel(be_ref, nu_ref, h_ref, w_ref, b_ref, y_ref):
    i = pl.program_id(0)

    @pl.when(i < nu_ref[0])
    def _():
        y_ref[...] = _dot(h_ref[...], w_ref[0]) + b_ref[0]

    @pl.when(i >= nu_ref[0])
    def _():
        y_ref[...] = jnp.zeros_like(y_ref)


def _moe_down(hs, blk_exp, n_used, wd_bf, bd, tm):
    n, f = hs.shape
    n_exp, _, d = wd_bf.shape
    vmem = 2 * (tm * f * 2 + f * d * 2 + tm * d * 4) + 2 * tm * d * 4 + (4 << 20)
    grid_spec = pltpu.PrefetchScalarGridSpec(
        num_scalar_prefetch=2,
        grid=(n // tm,),
        in_specs=[pl.BlockSpec((tm, f), lambda i, be, nu: (i, 0)),
                  pl.BlockSpec((1, f, d), lambda i, be, nu: (be[i], 0, 0)),
                  pl.BlockSpec((1, 1, d), lambda i, be, nu: (be[i], 0, 0))],
        out_specs=pl.BlockSpec((tm, d), lambda i, be, nu: (i, 0)))
    return pl.pallas_call(
        _moe_down_kernel,
        grid_spec=grid_spec,
        out_shape=jax.ShapeDtypeStruct((n, d), F32),
        compiler_params=_params(("arbitrary",), vmem),
        name="moe_down",
    )(blk_exp, n_used, hs, wd_bf, bd.reshape(n_exp, 1, d))


def _combine_kernel(pos0_ref, pos1_ref, pos2_ref, y_ref, gate_ref, x1_ref, g2_ref, lg_ref, lb_ref,
                    o_ref, buf_ref, sems, *, tm, n_steps):
    i = pl.program_id(0)
    slot, start_ahead = _ring_gather_step(i, n_steps, (pos0_ref, pos1_ref, pos2_ref), TOP_K * tm, y_ref,
                                          buf_ref, sems, True)
    start_ahead()
    gates = gate_ref[...]
    moe = gates[:, 0:1] * buf_ref[slot, 0:tm, :]
    for k in range(1, TOP_K):
        moe = moe + gates[:, k:k + 1] * buf_ref[slot, k * tm:(k + 1) * tm, :]
    pre = DN_ALPHA * x1_ref[...] + g2_ref[0] * moe
    o_ref[...] = _layer_norm_rows(pre, lg_ref[...], lb_ref[...])
    _drain_ring(i, n_steps, buf_ref, sems)


def _combine(y, pos, gates_t, x1, modv, ln_g, ln_b, rows_per_batch, g_col):
    r, d = x1.shape
    tm = _tile(rows_per_batch, 128)
    tiles_per_batch = rows_per_batch // tm
    n_steps = r // tm
    modv3 = modv.reshape(modv.shape[0], 1, modv.shape[1])
    pos_steps = pos.reshape(TOP_K, n_steps, tm).transpose(1, 0, 2).reshape(n_steps, 1, TOP_K * tm)
    kern = functools.partial(_combine_kernel, tm=tm, n_steps=n_steps)
    vmem = GATHER_SLOTS * TOP_K * tm * d * 4 + 4 * tm * d * 4 + 4 * tm * d * 4 + (4 << 20)
    pos_specs = _gather_index_specs(
        TOP_K * tm, n_steps, lambda a: (lambda i: (jnp.minimum(i + a, n_steps - 1), 0, 0)))
    return pl.pallas_call(
        kern,
        grid=(n_steps,),
        in_specs=pos_specs + [
            pl.BlockSpec(memory_space=pl.ANY),
            pl.BlockSpec((tm, TOP_K), lambda i: (i, 0)),
            pl.BlockSpec((tm, d), lambda i: (i, 0)),
            pl.BlockSpec((1, 1, d), lambda i: (i // tiles_per_batch, 0, g_col)),
            pl.BlockSpec((1, d), lambda i: (0, 0)),
            pl.BlockSpec((1, d), lambda i: (0, 0))],
        out_specs=pl.BlockSpec((tm, d), lambda i: (i, 0)),
        out_shape=jax.ShapeDtypeStruct((r, d), F32),
        scratch_shapes=[pltpu.VMEM((GATHER_SLOTS, TOP_K * tm, d), F32),
                        pltpu.SemaphoreType.DMA((GATHER_SLOTS,))],
        compiler_params=_params(("arbitrary",), vmem),
        name="combine",
    )(pos_steps, pos_steps, pos_steps, y, gates_t, x1, modv3, ln_g.reshape(1, d), ln_b.reshape(1, d))


def kernel(x, c, ctx, c_ctx, w_mod, b_mod, w_in, conv_w, conv_b, lru_wa, lru_ba, lru_wx, lru_bx,
           lru_lambda, sg_ln_g, sg_ln_b, sg_w, sg_b, w_branch_a, w_branch_b, w_out, ln1_g, ln1_b,
           w_router, b_router, w_gate_up, b_gate_up, w_down, b_down, ln2_g, ln2_b):
    assert w_mod.shape[0] == DEPTH
    batch, seq, d = x.shape
    ctx_len = ctx.shape[1]
    w_lru = conv_w.shape[-1]
    n_heads = lru_wa.shape[2]
    hd = w_lru // n_heads
    w_sg = sg_ln_g.shape[-1]
    n_exp = w_router.shape[-1]
    n_tok = batch * seq
    l = 0

    pad = (-(batch + 1)) % SUBLANES
    c_rows = jnp.concatenate([c, c_ctx[None, :], jnp.zeros((pad, d), F32)], axis=0)
    modv = _mod(c_rows, w_mod[l], b_mod[l])

    w_in_bf = w_in[l].astype(BF16)
    n_in = w_in_bf.shape[1]
    col_gy, col_u, col_v = w_lru, 2 * w_lru, 2 * w_lru + w_sg
    col_ga = 2 * w_lru + 2 * w_sg
    col_gb = col_ga + d
    x2 = x.reshape(n_tok, d)
    z = _inproj(x2, modv, w_in_bf, seq, 0, n_in, col_gy, col_ga)
    zc = _inproj(ctx.reshape(batch * ctx_len, d), modv, w_in_bf, batch * ctx_len, batch,
                 w_lru, w_lru, w_lru)

    wax = jnp.concatenate([lru_wa[l], lru_wx[l]], axis=-1).astype(BF16)
    y_a, wgu = _lru(z, zc, conv_w[l], conv_b[l], wax, lru_ba[l].reshape(2, 1, w_lru),
                    lru_bx[l].reshape(2, 1, w_lru), lru_lambda[l].reshape(2, 1, w_lru), w_gate_up[l],
                    batch, seq, ctx_len, w_lru, hd, col_gy)
    y_b = _spatial_gating(z, sg_ln_g[l], sg_ln_b[l], sg_w[l].astype(BF16), sg_b[l].T,
                          n_tok, w_sg, col_u, col_v)

    merged = _merge(y_a, y_b, w_branch_a[l].astype(BF16), w_branch_b[l].astype(BF16), z, col_ga, col_gb)
    r1 = _outproj(merged, w_out[l].astype(BF16), x2, modv, seq, 2)
    x1, tokens, idx_t, gate_t, rank_t, counts = _ln_route(
        r1, ln1_g[l], ln1_b[l], modv, w_router[l].T, b_router[l], seq, 4, 3)

    tm_moe = 256
    n_assign = n_tok * TOP_K
    n_tiles = -(-(n_assign + n_exp * (tm_moe - 1)) // tm_moe)
    n_pad = n_tiles * tm_moe
    cnt = counts[:, 0]
    padded = (cnt + tm_moe - 1) // tm_moe * tm_moe
    pad_end = jnp.cumsum(padded)
    pad_start = pad_end - padded
    expert_ids = jnp.arange(n_exp, dtype=jnp.int32)
    pos = rank_t + jnp.sum(jnp.where(idx_t[..., None] == expert_ids, pad_start, 0), axis=-1)
    tok_ids = jnp.broadcast_to(jnp.arange(n_tok, dtype=jnp.int32)[None, :], (TOP_K, n_tok))
    tok_pad = jnp.zeros((n_pad,), jnp.int32).at[pos.reshape(-1)].set(tok_ids.reshape(-1))
    blk_start = jnp.arange(n_tiles, dtype=jnp.int32) * tm_moe
    blk_exp = jnp.minimum(jnp.sum(blk_start[:, None] >= pad_end[None, :], axis=1), n_exp - 1).astype(jnp.int32)
    n_used = (pad_end[-1] // tm_moe).astype(jnp.int32).reshape(1)

    hs, wd_bf = _moe_up(tokens, tok_pad, blk_exp, n_used, wgu,
                        b_gate_up[l][..., ::2], b_gate_up[l][..., 1::2], w_down[l], n_assign // tm_moe, tm_moe)
    ys = _moe_down(hs, blk_exp, n_used, wd_bf, b_down[l], tm_moe)

    out = _combine(ys, pos, gate_t.T, x1, modv, ln2_g[l], ln2_b[l], seq, 5)
    return out.reshape(batch, seq, d)
```
